```python
import jax, jax.numpy as jnp
from jax import lax
import numpy as np

D_MODEL = 1024
BATCH = 32
SEQ = 256
DEPTH = 4
DEC_BATCH = 8
DEC_SEQ = 4096
PAST_LEN = 256

GRID_W = 64
N_MIXERS = 3
N_ATTN = (DEPTH + 2) // 3
N_CONV = (DEPTH + 1) // 3
N_RWKV = DEPTH // 3

HEAD_DIM = 64
N_Q_HEADS = D_MODEL // HEAD_DIM
N_KV_HEADS = max(1, N_Q_HEADS // 4)
Q_PER_KV = N_Q_HEADS // N_KV_HEADS
Q_DIM = N_Q_HEADS * HEAD_DIM
KV_DIM = N_KV_HEADS * HEAD_DIM
WINDOW = 128
BLOCK = 128
ROPE_BASE = 10000.0

CONV_WIDTH = 31
CONV_PAD = (CONV_WIDTH - 1) // 2

RWKV_HEAD = 64
RWKV_HEADS = D_MODEL // RWKV_HEAD
DECAY_LORA = max(32, int(round(1.8 * D_MODEL ** 0.5 / 32)) * 32)
AAA_LORA = max(32, int(round(1.8 * D_MODEL ** 0.5 / 32)) * 32)
GATE_LORA = max(32, int(round(0.6 * D_MODEL ** 0.8 / 32)) * 32)
GN_EPS = 64e-5

N_GROUPS = 4
EXPERTS_PER_GROUP = 8
N_EXPERTS = N_GROUPS * EXPERTS_PER_GROUP
TOP_K = 2
EXPERT_FF = D_MODEL // 2
MOE_BLOCK = 128

LN_EPS = 1e-5
ALPHA = (2 * DEPTH) ** 0.25
BETA = (8 * DEPTH) ** -0.25

kernel_name = "hybrid_dit_swa_conformer_rwkv7_hmoe_step"


def layer_norm(x, g, b, eps=LN_EPS):
    xf = x.astype(jnp.float32)
    mu = jnp.mean(xf, -1, keepdims=True)
    var = jnp.mean(jnp.square(xf - mu), -1, keepdims=True)
    return ((xf - mu) * lax.rsqrt(var + eps) * g + b).astype(x.dtype)


def modulation(cond, w_mod, b_mod):
    m = jax.nn.silu(cond) @ w_mod + b_mod
    return [t[:, None, :] for t in jnp.split(m, 6, axis=-1)]


def axial_rope(x):
    T = x.shape[1]
    rows = T // GRID_W
    row = jnp.repeat(jnp.arange(rows, dtype=jnp.float32), GRID_W)
    col = jnp.tile(jnp.arange(GRID_W, dtype=jnp.float32), rows)
    quarter = HEAD_DIM // 4
    half = HEAD_DIM // 2
    inv = ROPE_BASE ** (-jnp.arange(quarter, dtype=jnp.float32) / quarter)
    bshape = (T,) + (1,) * (x.ndim - 3) + (quarter,)
    xf = x.astype(jnp.float32)

    def rot(xh, pos):
        ang = (pos[:, None] * inv[None, :]).reshape(bshape)
        cs, sn = jnp.cos(ang), jnp.sin(ang)
        x1, x2 = xh[..., :quarter], xh[..., quarter:]
        return jnp.concatenate([x1 * cs - x2 * sn, x2 * cs + x1 * sn], -1)

    return jnp.concatenate([rot(xf[..., :half], row), rot(xf[..., half:], col)], -1).astype(x.dtype)


def attn_project(h, w_qkv):
    B, T, _ = h.shape
    qkv = h @ w_qkv
    q = qkv[..., :Q_DIM].reshape(B, T, N_KV_HEADS, Q_PER_KV, HEAD_DIM)
    k = qkv[..., Q_DIM:Q_DIM + KV_DIM].reshape(B, T, N_KV_HEADS, HEAD_DIM)
    v = qkv[..., Q_DIM + KV_DIM:].reshape(B, T, N_KV_HEADS, HEAD_DIM)
    return q, k, v


def sink_attend(s, vals, sink):
    sk = sink.astype(jnp.float32).reshape(1, N_KV_HEADS, Q_PER_KV, 1, 1)
    m = jnp.maximum(jnp.max(s, -1, keepdims=True), sk)
    p = jnp.exp(s - m)
    denom = jnp.sum(p, -1, keepdims=True) + jnp.exp(sk - m)
    return jnp.einsum("bhgqk,bkhd->bqhgd", p / denom, vals)


def _query_blocks(q):
    B, T = q.shape[:2]
    nb = T // BLOCK
    return jnp.moveaxis(q.reshape(B, nb, BLOCK, N_KV_HEADS, Q_PER_KV, HEAD_DIM), 1, 0)


def _merge_blocks(o, B, T):
    return jnp.moveaxis(o, 0, 1).reshape(B, T, Q_DIM)


def context_attention(q, k, v, sink):
    B, L = q.shape[:2]
    scale = HEAD_DIM ** -0.5
    kf = k.astype(jnp.float32)
    vf = v.astype(jnp.float32)

    def blk(qj):
        s = jnp.einsum("bqhgd,bkhd->bhgqk", qj.astype(jnp.float32), kf) * scale
        return sink_attend(s, vf, sink)

    return _merge_blocks(lax.map(blk, _query_blocks(q)), B, L)


def latent_attention(q, k, v, kc, vc, sink):
    B, T = q.shape[:2]
    nb = T // BLOCK
    scale = HEAD_DIM ** -0.5
    pad = ((0, 0), (BLOCK, BLOCK), (0, 0), (0, 0))
    kp = jnp.pad(k.astype(jnp.float32), pad)
    vp = jnp.pad(v.astype(jnp.float32), pad)
    kcf = kc.astype(jnp.float32)
    vcf = vc.astype(jnp.float32)
    offs = jnp.arange(3 * BLOCK) - BLOCK
    qi = jnp.arange(BLOCK)

    def blk(args):
        j, qj = args
        qj = qj.astype(jnp.float32)
        kj = lax.dynamic_slice_in_dim(kp, j * BLOCK, 3 * BLOCK, axis=1)
        vj = lax.dynamic_slice_in_dim(vp, j * BLOCK, 3 * BLOCK, axis=1)
        kpos = j * BLOCK + offs
        qpos = j * BLOCK + qi
        valid = (jnp.abs(qpos[:, None] - kpos[None, :]) <= WINDOW) & ((kpos >= 0) & (kpos < T))[None, :]
        s_loc = jnp.einsum("bqhgd,bkhd->bhgqk", qj, kj) * scale
        s_loc = jnp.where(valid, s_loc, -jnp.inf)
        s_ctx = jnp.einsum("bqhgd,bkhd->bhgqk", qj, kcf) * scale
        s = jnp.concatenate([s_ctx, s_loc], -1)
        vals = jnp.concatenate([vcf, vj], axis=1)
        return sink_attend(s, vals, sink)

    o = lax.map(blk, (jnp.arange(nb), _query_blocks(q)))
    return _merge_blocks(o, B, T)


def conformer_conv(h, w_pw1, b_pw1, w_dw, b_dw, ln_g, ln_b, w_pw2, b_pw2):
    u = h @ w_pw1 + b_pw1
    a, g = jnp.split(u, 2, axis=-1)
    u = a * jax.nn.sigmoid(g)
    u = lax.conv_general_dilated(u, w_dw[:, None, :], window_strides=(1,), padding=[(CONV_PAD, CONV_PAD)],
                                 dimension_numbers=("NWC", "WIO", "NWC"), feature_group_count=D_MODEL) + b_dw
    u = jax.nn.silu(layer_norm(u, ln_g, ln_b))
    return u @ w_pw2 + b_pw2


def token_shift_centered(x):
    prev = jnp.pad(x[:, :-1], ((0, 0), (1, 0), (0, 0)))
    nxt = jnp.pad(x[:, 1:], ((0, 0), (0, 1), (0, 0)))
    return 0.5 * (prev + nxt)


def rwkv_scan(s0, r, decay, k, v, a, b, reverse):
    def step(S, inp):
        r_t, w_t, k_t, v_t, a_t, b_t = inp
        sa = jnp.einsum("bhij,bhj->bhi", S, a_t)
        S = S * w_t[:, :, None, :] + sa[..., None] * b_t[:, :, None, :] + v_t[..., None] * k_t[:, :, None, :]
        return S, jnp.einsum("bhij,bhj->bhi", S, r_t)

    xs = tuple(jnp.moveaxis(t, 1, 0) for t in (r, decay, k, v, a, b))
    s_final, ys = lax.scan(step, s0, xs, reverse=reverse)
    return s_final, jnp.moveaxis(ys, 0, 1)


def rwkv_mixer(h, s0_f, s0_b, mu, w_rkv, w0, w1, w2, a0, a1, a2, g1, g2, k_k, k_a, r_k, gn_g, gn_b, w_o):
    f32 = jnp.float32
    B, T, D = h.shape
    hs = (B, T, RWKV_HEADS, RWKV_HEAD)
    xx = token_shift_centered(h) - h
    xr, xw, xk, xv, xa, xg = [h + xx * mu[j] for j in range(6)]
    r = (xr @ w_rkv[0]).astype(f32).reshape(hs)
    k = (xk @ w_rkv[1]).astype(f32)
    v = (xv @ w_rkv[2]).astype(f32).reshape(hs)
    g = (jax.nn.sigmoid(xg @ g1) @ g2).astype(f32)
    kk = (k * k_k).reshape(hs)
    kk = kk / jnp.maximum(jnp.sqrt(jnp.sum(kk * kk, -1, keepdims=True)), 1e-12)
    outs, finals, bonuses = [], [], []
    for d, (s0, rev) in enumerate(((s0_f, False), (s0_b, True))):
        w_log = -jax.nn.softplus(-(w0[d] + jnp.tanh(xw @ w1[d]) @ w2[d]).astype(f32)) - 0.5
        decay = jnp.exp(-jnp.exp(w_log)).reshape(hs)
        a = jax.nn.sigmoid((a0[d] + (xa @ a1[d]) @ a2[d]).astype(f32))
        kd = (k * (1.0 + (a - 1.0) * k_a)).reshape(hs)
        b = kk * a.reshape(hs)
        s_fin, y = rwkv_scan(s0.astype(f32), r, decay, kd, v, -kk, b, rev)
        outs.append(y)
        finals.append(s_fin)
        bonuses.append(jnp.sum(r * kd * r_k, -1, keepdims=True) * v)
    y = outs[0] + outs[1]
    mu_y = jnp.mean(y, -1, keepdims=True)
    var_y = jnp.mean(jnp.square(y - mu_y), -1, keepdims=True)
    y = (y - mu_y) * lax.rsqrt(var_y + GN_EPS) * gn_g.reshape(RWKV_HEADS, RWKV_HEAD) + gn_b.reshape(RWKV_HEADS, RWKV_HEAD)
    y = (y + bonuses[0] + bonuses[1]).reshape(B, T, D) * g
    return y.astype(h.dtype) @ w_o, finals[0], finals[1]


def grouped_experts(x, e_idx, gate_w, w_gu, w_down):
    N, D = x.shape
    S = N * TOP_K
    e_flat = e_idx.reshape(-1).astype(jnp.int32)
    order = jnp.argsort(e_flat).astype(jnp.int32)
    e_sorted = e_flat[order]
    counts = jnp.bincount(e_flat, length=N_EXPERTS).astype(jnp.int32)
    padded = (counts + MOE_BLOCK - 1) // MOE_BLOCK * MOE_BLOCK
    pad_end = jnp.cumsum(padded)
    pad_start = pad_end - padded
    start = jnp.cumsum(counts) - counts
    dest_sorted = (pad_start[e_sorted] + jnp.arange(S, dtype=jnp.int32) - start[e_sorted]).astype(jnp.int32)
    n_blocks = -(-S // MOE_BLOCK) + N_EXPERTS
    P = n_blocks * MOE_BLOCK
    slot_token = jnp.full((P,), N, jnp.int32).at[dest_sorted].set(order // TOP_K)
    x_pad = jnp.concatenate([x, jnp.zeros((1, D), x.dtype)], 0)
    xb = x_pad[slot_token].reshape(n_blocks, MOE_BLOCK, D)
    block_expert = jnp.minimum(jnp.searchsorted(pad_end, jnp.arange(n_blocks, dtype=jnp.int32) * MOE_BLOCK, side="right"),
                               N_EXPERTS - 1)

    def run(args):
        xj, e = args
        gt, up = jnp.split(xj @ w_gu[e], 2, axis=-1)
        return (jax.nn.silu(gt) * up) @ w_down[e]

    yb = lax.map(run, (xb, block_expert)).reshape(P, D)
    dest = jnp.zeros((S,), jnp.int32).at[order].set(dest_sorted)
    y = yb[dest].reshape(N, TOP_K, D)
    return jnp.einsum("nk,nkd->nd", gate_w.astype(y.dtype), y)


def hier_moe(h, w_group, b_group, w_expert, b_expert, w_gu, w_down):
    B, T, D = h.shape
    x = h.reshape(B * T, D)
    xf = x.astype(jnp.float32)
    g_logits = xf @ w_group.astype(jnp.float32) + b_group.astype(jnp.float32)
    g_prob = jax.nn.softmax(g_logits, -1)
    _, g_idx = lax.top_k(g_logits, 1)
    p_g = jnp.take_along_axis(g_prob, g_idx, -1)
    e_logits = (xf @ w_expert.astype(jnp.float32) + b_expert.astype(jnp.float32)).reshape(-1, N_GROUPS, EXPERTS_PER_GROUP)
    e_logits = jnp.take_along_axis(e_logits, g_idx[:, :, None], axis=1)[:, 0]
    top_l, top_i = lax.top_k(e_logits, TOP_K)
    gate_w = jax.nn.softmax(top_l, -1) * p_g
    e_idx = g_idx * EXPERTS_PER_GROUP + top_i
    y = grouped_experts(x, e_idx, gate_w, w_gu, w_down)
    return y.reshape(B, T, D).astype(h.dtype)


def setup_inputs(seed: int = 0) -> dict:
    key = jax.random.key(seed)
    ks = iter(jax.random.split(key, 64))
    D = D_MODEL

    def nrm(shape, scale):
        return jax.random.normal(next(ks), shape, jnp.float32) * scale

    def gain(shape):
        return 1.0 + nrm(shape, 0.02)

    return {
        "x_prompt": nrm((BATCH, SEQ, D), 1.0),
        "x_sample": nrm((DEC_BATCH, DEC_SEQ, D), 1.0),
        "c": nrm((DEC_BATCH, D), 1.0),
        "c_ctx": nrm((D,), 1.0),
        "cache_attn_k": nrm((DEC_BATCH, N_ATTN, PAST_LEN, N_KV_HEADS, HEAD_DIM), 1.0),
        "cache_attn_v": nrm((DEC_BATCH, N_ATTN, PAST_LEN, N_KV_HEADS, HEAD_DIM), 1.0),
        "state_rwkv_fwd": nrm((DEC_BATCH, N_RWKV, RWKV_HEADS, RWKV_HEAD, RWKV_HEAD), 0.5),
        "state_rwkv_bwd": nrm((DEC_BATCH, N_RWKV, RWKV_HEADS, RWKV_HEAD, RWKV_HEAD), 0.5),
        "mod_w": nrm((DEPTH, D, 6 * D), D ** -0.5),
        "mod_b": nrm((DEPTH, 6 * D), 0.02),
        "post_ln_g": gain((DEPTH, 2, D)),
        "post_ln_b": nrm((DEPTH, 2, D), 0.02),
        "attn_w_qkv": nrm((N_ATTN, D, Q_DIM + 2 * KV_DIM), D ** -0.5),
        "attn_w_o": nrm((N_ATTN, Q_DIM, D), BETA * Q_DIM ** -0.5),
        "attn_sink": nrm((N_ATTN, N_Q_HEADS), 0.5),
        "conv_w_pw1": nrm((N_CONV, D, 2 * D), D ** -0.5),
        "conv_b_pw1": nrm((N_CONV, 2 * D), 0.02),
        "conv_w_dw": nrm((N_CONV, CONV_WIDTH, D), CONV_WIDTH ** -0.5),
        "conv_b_dw": nrm((N_CONV, D), 0.02),
        "conv_ln_g": gain((N_CONV, D)),
        "conv_ln_b": nrm((N_CONV, D), 0.02),
        "conv_w_pw2": nrm((N_CONV, D, D), BETA * D ** -0.5),
        "conv_b_pw2": nrm((N_CONV, D), 0.02),
        "rwkv_mu": jax.random.uniform(next(ks), (N_RWKV, 6, D), jnp.float32),
        "rwkv_w_rkv": nrm((N_RWKV, 3, D, D), D ** -0.5),
        "rwkv_w0": jnp.linspace(-6.0, -1.0, D, dtype=jnp.float32)[None, None, :] + nrm((N_RWKV, 2, D), 0.1),
        "rwkv_w1": nrm((N_RWKV, 2, D, DECAY_LORA), D ** -0.5),
        "rwkv_w2": nrm((N_RWKV, 2, DECAY_LORA, D), 0.1 * DECAY_LORA ** -0.5),
        "rwkv_a0": nrm((N_RWKV, 2, D), 0.1),
        "rwkv_a1": nrm((N_RWKV, 2, D, AAA_LORA), D ** -0.5),
        "rwkv_a2": nrm((N_RWKV, 2, AAA_LORA, D), 0.5 * AAA_LORA ** -0.5),
        "rwkv_g1": nrm((N_RWKV, D, GATE_LORA), D ** -0.5),
        "rwkv_g2": nrm((N_RWKV, GATE_LORA, D), GATE_LORA ** -0.5),
        "rwkv_k_k": 0.85 + nrm((N_RWKV, D), 0.02),
        "rwkv_k_a": gain((N_RWKV, D)),
        "rwkv_r_k": nrm((N_RWKV, RWKV_HEADS, RWKV_HEAD), 0.1),
        "rwkv_gn_g": gain((N_RWKV, D)),
        "rwkv_gn_b": nrm((N_RWKV, D), 0.02),
        "rwkv_w_o": nrm((N_RWKV, D, D), BETA * D ** -0.5),
        "moe_w_group": nrm((DEPTH, D, N_GROUPS), D ** -0.5),
        "moe_b_group": nrm((DEPTH, N_GROUPS), 0.01),
        "moe_w_expert": nrm((DEPTH, D, N_EXPERTS), D ** -0.5),
        "moe_b_expert": nrm((DEPTH, N_EXPERTS), 0.01),
        "moe_w_gate_up": nrm((DEPTH, N_EXPERTS, D, 2 * EXPERT_FF), D ** -0.5),
        "moe_w_down": nrm((DEPTH, N_EXPERTS, EXPERT_FF, D), BETA * EXPERT_FF ** -0.5),
    }


def reference(x_prompt, x_sample, c, c_ctx, cache_attn_k, cache_attn_v, state_rwkv_fwd, state_rwkv_bwd,
              mod_w, mod_b, post_ln_g, post_ln_b,
              attn_w_qkv, attn_w_o, attn_sink,
              conv_w_pw1, conv_b_pw1, conv_w_dw, conv_b_dw, conv_ln_g, conv_ln_b, conv_w_pw2, conv_b_pw2,
              rwkv_mu, rwkv_w_rkv, rwkv_w0, rwkv_w1, rwkv_w2, rwkv_a0, rwkv_a1, rwkv_a2, rwkv_g1, rwkv_g2,
              rwkv_k_k, rwkv_k_a, rwkv_r_k, rwkv_gn_g, rwkv_gn_b, rwkv_w_o,
              moe_w_group, moe_b_group, moe_w_expert, moe_b_expert, moe_w_gate_up, moe_w_down):
    xp, xs = x_prompt, x_sample
    new_k, new_v, new_sf, new_sb = [], [], [], []
    n_attn = n_conv = n_rwkv = 0
    for i in range(DEPTH):
        mp = modulation(c_ctx[None, :], mod_w[i], mod_b[i])
        ms = modulation(c, mod_w[i], mod_b[i])
        hp = xp * (1.0 + mp[1]) + mp[0]
        hs = xs * (1.0 + ms[1]) + ms[0]
        kind = i % N_MIXERS
        if kind == 0:
            j = n_attn
            qp, kp_, vp_ = attn_project(hp, attn_w_qkv[j])
            op = context_attention(qp, kp_, vp_, attn_sink[j]).astype(hp.dtype) @ attn_w_o[j]
            qs, ks_, vs_ = attn_project(hs, attn_w_qkv[j])
            os_ = latent_attention(axial_rope(qs), axial_rope(ks_), vs_, cache_attn_k[:, j], cache_attn_v[:, j],
                                   attn_sink[j]).astype(hs.dtype) @ attn_w_o[j]
            new_k.append(kp_)
            new_v.append(vp_)
            n_attn += 1
        elif kind == 1:
            j = n_conv
            cw = (conv_w_pw1[j], conv_b_pw1[j], conv_w_dw[j], conv_b_dw[j], conv_ln_g[j], conv_ln_b[j],
                  conv_w_pw2[j], conv_b_pw2[j])
            op = conformer_conv(hp, *cw)
            os_ = conformer_conv(hs, *cw)
            n_conv += 1
        else:
            j = n_rwkv
            rw = (rwkv_mu[j], rwkv_w_rkv[j], rwkv_w0[j], rwkv_w1[j], rwkv_w2[j], rwkv_a0[j], rwkv_a1[j],
                  rwkv_a2[j], rwkv_g1[j], rwkv_g2[j], rwkv_k_k[j], rwkv_k_a[j], rwkv_r_k[j], rwkv_gn_g[j],
                  rwkv_gn_b[j], rwkv_w_o[j])
            s_zero = jnp.zeros((hp.shape[0], RWKV_HEADS, RWKV_HEAD, RWKV_HEAD), jnp.float32)
            op, sf, sb = rwkv_mixer(hp, s_zero, s_zero, *rw)
            os_, _, _ = rwkv_mixer(hs, state_rwkv_fwd[:, j], state_rwkv_bwd[:, j], *rw)
            new_sf.append(sf)
            new_sb.append(sb)
            n_rwkv += 1
        xp = layer_norm(ALPHA * xp + mp[2] * op, post_ln_g[i, 0], post_ln_b[i, 0])
        xs = layer_norm(ALPHA * xs + ms[2] * os_, post_ln_g[i, 0], post_ln_b[i, 0])
        moe = (moe_w_group[i], moe_b_group[i], moe_w_expert[i], moe_b_expert[i], moe_w_gate_up[i], moe_w_down[i])
        hp = xp * (1.0 + mp[4]) + mp[3]
        hs = xs * (1.0 + ms[4]) + ms[3]
        xp = layer_norm(ALPHA * xp + mp[5] * hier_moe(hp, *moe), post_ln_g[i, 1], post_ln_b[i, 1])
        xs = layer_norm(ALPHA * xs + ms[5] * hier_moe(hs, *moe), post_ln_g[i, 1], post_ln_b[i, 1])
    new_cache_attn_k = jnp.stack(new_k, axis=1)
    new_cache_attn_v = jnp.stack(new_v, axis=1)
    new_state_rwkv_fwd = jnp.stack(new_sf, axis=1).astype(x_prompt.dtype)
    new_state_rwkv_bwd = jnp.stack(new_sb, axis=1).astype(x_prompt.dtype)
    return (xp, xs, new_cache_attn_k, new_cache_attn_v, new_state_rwkv_fwd, new_state_rwkv_bwd)
```

```python
import functools

import jax
import jax.numpy as jnp
from jax import lax
from jax.experimental import pallas as pl
from jax.experimental.pallas import tpu as pltpu

f32 = jnp.float32
bf16 = jnp.bfloat16

D = 1024
DEPTH = 4
GRID_W = 64
HEAD_DIM = 64
N_Q_HEADS = D // HEAD_DIM
N_KV_HEADS = N_Q_HEADS // 4
Q_PER_KV = N_Q_HEADS // N_KV_HEADS
Q_DIM = N_Q_HEADS * HEAD_DIM
KV_DIM = N_KV_HEADS * HEAD_DIM
WINDOW = 128
ROPE_BASE = 10000.0
CONV_WIDTH = 31
CONV_PAD = (CONV_WIDTH - 1) // 2
RWKV_HEAD = 64
RWKV_HEADS = D // RWKV_HEAD
GN_EPS = 64e-5
N_GROUPS = 4
EXPERTS_PER_GROUP = 8
N_EXPERTS = N_GROUPS * EXPERTS_PER_GROUP
TOP_K = 2
EXPERT_FF = D // 2
LN_EPS = 1e-5
ALPHA = (2 * DEPTH) ** 0.25

LANES = 128
ROW_TILE = 256
MOE_ROWS = 256
ATT_Q = 128
SCAN_CHUNK = 64
SCAN_ROWS = 256
MOD_ROWS = 16
VMEM_LIMIT = 56 * 1024 * 1024
NEG = -1e30
HI = lax.Precision.HIGHEST


def _params(n_axes):
    return pltpu.CompilerParams(dimension_semantics=("arbitrary",) * n_axes,
                                vmem_limit_bytes=VMEM_LIMIT)


def _const_spec(shape):
    nd = len(shape)
    return pl.BlockSpec(shape, lambda *_: (0,) * nd)


class _Layout:
    def __init__(self, bp, sp, bs, ss):
        self.bp, self.sp, self.bs, self.ss = bp, sp, bs, ss
        self.n_prompt = bp * sp
        self.n = bp * sp + bs * ss
        assert sp % ROW_TILE == 0 and ss % ROW_TILE == 0
        self.n_tiles = self.n // ROW_TILE
        self.prompt_tiles = self.n_prompt // ROW_TILE
        self.tiles_per_sseq = ss // ROW_TILE
        self.tiles_per_pseq = sp // ROW_TILE

    def mod_row(self, i):
        return jnp.where(i < self.prompt_tiles, 0, 1 + (i - self.prompt_tiles) // self.tiles_per_sseq)

    def seq_pos(self, i):
        in_p = i < self.prompt_tiles
        pos = jnp.where(in_p, i % self.tiles_per_pseq, (i - self.prompt_tiles) % self.tiles_per_sseq)
        cnt = jnp.where(in_p, self.tiles_per_pseq, self.tiles_per_sseq)
        return pos, cnt

    def row_spec(self, width=D, col=0):
        return pl.BlockSpec((ROW_TILE, width), lambda i: (i, col))

    def mod_spec(self):
        return pl.BlockSpec((None, 1, D), lambda i: (self.mod_row(i), 0, 0))


def _layer_norm(z, g, b, eps=LN_EPS):
    mu = jnp.mean(z, axis=-1, keepdims=True)
    zc = z - mu
    var = jnp.mean(zc * zc, axis=-1, keepdims=True)
    return zc * lax.rsqrt(var + eps) * g + b


def _silu(x):
    return x * jax.nn.sigmoid(x)


def _mod_body(c_ref, w_ref, b_ref, o_ref):
    o_ref[...] = jnp.dot(_silu(c_ref[...]), w_ref[...], precision=HI,
                         preferred_element_type=f32) + b_ref[...]


def _modulation_table(cond, mod_w, mod_b):
    out = pl.pallas_call(
        _mod_body,
        grid=(DEPTH, 6),
        in_specs=[_const_spec((MOD_ROWS, D)),
                  pl.BlockSpec((None, D, D), lambda l, j: (l, 0, j)),
                  pl.BlockSpec((None, 1, D), lambda l, j: (l, 0, j))],
        out_specs=pl.BlockSpec((None, None, MOD_ROWS, D), lambda l, j: (l, j, 0, 0)),
        out_shape=jax.ShapeDtypeStruct((DEPTH, 6, MOD_ROWS, D), f32),
        compiler_params=_params(2),
        name="modulation",
    )(cond, mod_w, mod_b.reshape(DEPTH, 1, 6 * D))
    return out.reshape(DEPTH, 6, MOD_ROWS, 1, D)


def _proj_body(x_ref, sh_ref, sc_ref, w_ref, b_ref, o_ref, *, glu):
    h = x_ref[...] * (1.0 + sc_ref[...]) + sh_ref[...]
    u = jnp.dot(h.astype(bf16), w_ref[...], preferred_element_type=f32) + b_ref[...]
    if glu:
        half = u.shape[1] // 2
        u = u[:, :half] * jax.nn.sigmoid(u[:, half:])
    o_ref[...] = u.astype(o_ref.dtype)


def _mod_proj(lay, x, shift, scale, w, b, *, glu=False, out_dtype=f32, name="proj"):
    k, n = w.shape
    n_out = n // 2 if glu else n
    return pl.pallas_call(
        functools.partial(_proj_body, glu=glu),
        grid=(lay.n_tiles,),
        in_specs=[lay.row_spec(), lay.mod_spec(), lay.mod_spec(),
                  _const_spec((k, n)), _const_spec((1, n))],
        out_specs=lay.row_spec(n_out),
        out_shape=jax.ShapeDtypeStruct((lay.n, n_out), out_dtype),
        compiler_params=_params(1),
        name=name,
    )(x, shift, scale, w.astype(bf16), b.reshape(1, n))


def _out_ln_body(y_ref, w_ref, b_ref, x_ref, gate_ref, g_ref, be_ref, o_ref):
    t = jnp.dot(y_ref[...].astype(bf16), w_ref[...], preferred_element_type=f32) + b_ref[...]
    z = ALPHA * x_ref[...] + gate_ref[...] * t
    o_ref[...] = _layer_norm(z, g_ref[...], be_ref[...])


def _out_proj_ln(lay, y, w, b, x, gate, ln_g, ln_b, name="out_ln"):
    k = w.shape[0]
    return pl.pallas_call(
        _out_ln_body,
        grid=(lay.n_tiles,),
        in_specs=[lay.row_spec(k), _const_spec((k, D)), _const_spec((1, D)),
                  lay.row_spec(), lay.mod_spec(), _const_spec((1, D)), _const_spec((1, D))],
        out_specs=lay.row_spec(),
        out_shape=jax.ShapeDtypeStruct((lay.n, D), f32),
        compiler_params=_params(1),
        name=name,
    )(y, w.astype(bf16), b.reshape(1, D), x, gate, ln_g.reshape(1, D), ln_b.reshape(1, D))


def _sink_column(sink_ref, h, rows_per_head):
    r = lax.broadcasted_iota(jnp.int32, (Q_PER_KV * rows_per_head, 1), 0) // rows_per_head
    col = jnp.full(r.shape, sink_ref[h * Q_PER_KV], f32)
    for g in range(1, Q_PER_KV):
        col = jnp.where(r == g, sink_ref[h * Q_PER_KV + g], col)
    return col


def _stack_heads(q, h):
    return jnp.concatenate(
        [q[:, (h * Q_PER_KV + g) * HEAD_DIM:(h * Q_PER_KV + g + 1) * HEAD_DIM] for g in range(Q_PER_KV)], axis=0)


def _nt(a, b):
    return lax.dot_general(a, b, (((1,), (1,)), ((), ())), preferred_element_type=f32)


def _ctx_attn_body(sink_ref, q_ref, k_ref, v_ref, o_ref):
    rows = q_ref.shape[0]
    q = q_ref[...] * (HEAD_DIM ** -0.5)
    k = k_ref[...]
    v = v_ref[...]
    outs = [None] * N_Q_HEADS
    for h in range(N_KV_HEADS):
        qh = _stack_heads(q, h).astype(bf16)
        kh = k[:, h * HEAD_DIM:(h + 1) * HEAD_DIM].astype(bf16)
        vh = v[:, h * HEAD_DIM:(h + 1) * HEAD_DIM].astype(bf16)
        s = _nt(qh, kh)
        sk = _sink_column(sink_ref, h, rows)
        m = jnp.maximum(jnp.max(s, axis=-1, keepdims=True), sk)
        p = jnp.exp(s - m)
        den = jnp.sum(p, axis=-1, keepdims=True) + jnp.exp(sk - m)
        o = jnp.dot(p.astype(bf16), vh, preferred_element_type=f32) / den
        for g in range(Q_PER_KV):
            outs[h * Q_PER_KV + g] = o[g * rows:(g + 1) * rows]
    o_ref[...] = jnp.concatenate(outs, axis=1).astype(o_ref.dtype)


def _context_attention(lay, qkv, sink):
    sp = lay.sp
    kcol = Q_DIM // KV_DIM
    return pl.pallas_call(
        _ctx_attn_body,
        grid=(lay.bp,),
        in_specs=[pl.BlockSpec(memory_space=pltpu.SMEM),
                  pl.BlockSpec((sp, Q_DIM), lambda b: (b, 0)),
                  pl.BlockSpec((sp, KV_DIM), lambda b: (b, kcol)),
                  pl.BlockSpec((sp, KV_DIM), lambda b: (b, kcol + 1))],
        out_specs=pl.BlockSpec((sp, Q_DIM), lambda b: (b, 0)),
        out_shape=jax.ShapeDtypeStruct((lay.n_prompt, Q_DIM), bf16),
        compiler_params=_params(1),
        name="ctx_attn",
    )(sink, qkv, qkv, qkv)


def _rope_tables(t_len):
    t = jnp.arange(t_len)
    row = (t // GRID_W).astype(f32)
    col = (t % GRID_W).astype(f32)
    quarter = HEAD_DIM // 4
    d = jnp.arange(LANES) % HEAD_DIM
    inv = ROPE_BASE ** (-(d % quarter).astype(f32) / quarter)
    pos = jnp.where((d < HEAD_DIM // 2)[None, :], row[:, None], col[:, None])
    ang = pos * inv[None, :]
    cs, sn = jnp.cos(ang), jnp.sin(ang)
    first = ((d % (2 * quarter)) < quarter)[None, :]
    return cs, jnp.where(first, -sn, 0.0), jnp.where(first, 0.0, sn)


def _rope(x, cs, sa, sb):
    quarter = HEAD_DIM // 4
    outs = []
    for c in range(x.shape[1] // LANES):
        xc = x[:, c * LANES:(c + 1) * LANES]
        outs.append(xc * cs + pltpu.roll(xc, LANES - quarter, axis=1) * sa + pltpu.roll(xc, quarter, axis=1) * sb)
    return outs[0] if len(outs) == 1 else jnp.concatenate(outs, axis=1)


def _lat_attn_body(sink_ref, q_ref, k_ref, v_ref, kc_ref, vc_ref, cs_ref, sa_ref, sb_ref, o_ref):
    j = pl.program_id(1)
    t_len = k_ref.shape[0]
    win = 3 * ATT_Q
    start = pl.multiple_of(jnp.clip((j - 1) * ATT_Q, 0, t_len - win), ATT_Q)
    q0 = pl.multiple_of(j * ATT_Q, ATT_Q)
    q = _rope(q_ref[...], cs_ref[pl.ds(q0, ATT_Q), :], sa_ref[pl.ds(q0, ATT_Q), :],
              sb_ref[pl.ds(q0, ATT_Q), :]) * (HEAD_DIM ** -0.5)
    kw = _rope(k_ref[pl.ds(start, win), :], cs_ref[pl.ds(start, win), :], sa_ref[pl.ds(start, win), :],
               sb_ref[pl.ds(start, win), :])
    vw = v_ref[pl.ds(start, win), :]
    kc = kc_ref[...]
    vc = vc_ref[...]
    rows = Q_PER_KV * ATT_Q
    qpos = q0 + lax.broadcasted_iota(jnp.int32, (rows, win), 0) % ATT_Q
    kpos = start + lax.broadcasted_iota(jnp.int32, (rows, win), 1)
    valid = jnp.abs(qpos - kpos) <= WINDOW
    outs = [None] * N_Q_HEADS
    for h in range(N_KV_HEADS):
        sl = slice(h * HEAD_DIM, (h + 1) * HEAD_DIM)
        qh = _stack_heads(q, h).astype(bf16)
        s_c = _nt(qh, kc[:, sl].astype(bf16))
        s_l = jnp.where(valid, _nt(qh, kw[:, sl].astype(bf16)), NEG)
        sk = _sink_column(sink_ref, h, ATT_Q)
        m = jnp.maximum(jnp.maximum(jnp.max(s_c, axis=-1, keepdims=True),
                                    jnp.max(s_l, axis=-1, keepdims=True)), sk)
        p_c = jnp.exp(s_c - m)
        p_l = jnp.exp(s_l - m)
        den = jnp.sum(p_c, axis=-1, keepdims=True) + jnp.sum(p_l, axis=-1, keepdims=True) + jnp.exp(sk - m)
        o = (jnp.dot(p_c.astype(bf16), vc[:, sl].astype(bf16), preferred_element_type=f32)
             + jnp.dot(p_l.astype(bf16), vw[:, sl].astype(bf16), preferred_element_type=f32)) / den
        for g in range(Q_PER_KV):
            outs[h * Q_PER_KV + g] = o[g * ATT_Q:(g + 1) * ATT_Q]
    o_ref[...] = jnp.concatenate(outs, axis=1).astype(o_ref.dtype)


def _latent_attention(lay, qkv, kc, vc, sink):
    ss = lay.ss
    past = kc.shape[1]
    nq = ss // ATT_Q
    q_base = lay.n_prompt // ATT_Q
    s_base = lay.n_prompt // ss
    assert lay.n_prompt % ss == 0
    kcol = Q_DIM // KV_DIM
    cs, sa, sb = _rope_tables(ss)
    return pl.pallas_call(
        _lat_attn_body,
        grid=(lay.bs, nq),
        in_specs=[pl.BlockSpec(memory_space=pltpu.SMEM),
                  pl.BlockSpec((ATT_Q, Q_DIM), lambda b, j: (q_base + b * nq + j, 0)),
                  pl.BlockSpec((ss, KV_DIM), lambda b, j: (s_base + b, kcol)),
                  pl.BlockSpec((ss, KV_DIM), lambda b, j: (s_base + b, kcol + 1)),
                  pl.BlockSpec((None, past, KV_DIM), lambda b, j: (b, 0, 0)),
                  pl.BlockSpec((None, past, KV_DIM), lambda b, j: (b, 0, 0)),
                  _const_spec((ss, LANES)), _const_spec((ss, LANES)), _const_spec((ss, LANES))],
        out_specs=pl.BlockSpec((ATT_Q, Q_DIM), lambda b, j: (b * nq + j, 0)),
        out_shape=jax.ShapeDtypeStruct((lay.bs * ss, Q_DIM), bf16),
        compiler_params=_params(2),
        name="lat_attn",
    )(sink, qkv, qkv, qkv, kc, vc, cs, sa, sb)


CONV_HALO = 16
CONV_SUB = 32


def _dwconv_body(lay, u_ref, prev_ref, next_ref, w_ref, b_ref, g_ref, be_ref, o_ref, buf_ref):
    i = pl.program_id(0)
    pos, cnt = lay.seq_pos(i)
    buf_ref[pl.ds(0, CONV_HALO), :] = jnp.where(pos > 0, prev_ref[...], 0.0)
    buf_ref[pl.ds(CONV_HALO, ROW_TILE), :] = u_ref[...]
    buf_ref[pl.ds(CONV_HALO + ROW_TILE, CONV_HALO), :] = jnp.where(pos < cnt - 1, next_ref[...], 0.0)
    w = w_ref[...]
    off = CONV_HALO - CONV_PAD

    for s in range(ROW_TILE // CONV_SUB):
        r0 = s * CONV_SUB
        acc = jnp.zeros((CONV_SUB, D), f32)
        for t in range(CONV_WIDTH):
            acc = acc + buf_ref[pl.ds(r0 + off + t, CONV_SUB), :] * w[t:t + 1, :]
        z = _layer_norm(acc + b_ref[...], g_ref[...], be_ref[...])
        o_ref[pl.ds(r0, CONV_SUB), :] = _silu(z).astype(o_ref.dtype)


def _dwconv_ln_swish(lay, u, w_dw, b_dw, ln_g, ln_b):
    hpt = ROW_TILE // CONV_HALO
    last = lay.n // CONV_HALO - 1
    return pl.pallas_call(
        functools.partial(_dwconv_body, lay),
        grid=(lay.n_tiles,),
        in_specs=[lay.row_spec(),
                  pl.BlockSpec((CONV_HALO, D), lambda i: (jnp.maximum(i * hpt - 1, 0), 0)),
                  pl.BlockSpec((CONV_HALO, D), lambda i: (jnp.minimum((i + 1) * hpt, last), 0)),
                  _const_spec((CONV_WIDTH, D)), _const_spec((1, D)), _const_spec((1, D)), _const_spec((1, D))],
        out_specs=lay.row_spec(),
        out_shape=jax.ShapeDtypeStruct((lay.n, D), bf16),
        scratch_shapes=[pltpu.VMEM((ROW_TILE + 2 * CONV_HALO, D), f32)],
        compiler_params=_params(1),
        name="dwconv",
    )(u, u, u, w_dw, b_dw.reshape(1, D), ln_g.reshape(1, D), ln_b.reshape(1, D))


def _conformer(lay, x, m, w_pw1, b_pw1, w_dw, b_dw, ln_g, ln_b, w_pw2, b_pw2):
    u = _mod_proj(lay, x, m[0], m[1], w_pw1, b_pw1, glu=True, name="conv_pw1_glu")
    return _dwconv_ln_swish(lay, u, w_dw, b_dw, ln_g, ln_b)


LORA_PAD = 128
GATE_PAD = 256


def _bdot(a, w):
    return jnp.dot(a.astype(bf16), w, preferred_element_type=f32)


def _rwkv_proj_body(lay, x_ref, prev_ref, next_ref, sh_ref, sc_ref, mu_ref, wrkv_ref, w0_ref, w1_ref, w2_ref,
                    a0_ref, a1_ref, a2_ref, g1_ref, g2_ref, r_ref, k_ref, v_ref, g_ref, lw_ref, a_ref):
    i = pl.program_id(0)
    pos, cnt = lay.seq_pos(i)
    sc = 1.0 + sc_ref[...]
    sh = sh_ref[...]
    h = x_ref[...] * sc + sh
    hb = prev_ref.shape[0]
    h_prev = jnp.where(pos > 0, prev_ref[hb - 1:hb, :] * sc + sh, 0.0)
    h_next = jnp.where(pos < cnt - 1, next_ref[0:1, :] * sc + sh, 0.0)
    row = lax.broadcasted_iota(jnp.int32, (ROW_TILE, 1), 0)
    below = jnp.where(row == 0, h_prev, pltpu.roll(h, 1, axis=0))
    above = jnp.where(row == ROW_TILE - 1, h_next, pltpu.roll(h, ROW_TILE - 1, axis=0))
    xx = 0.5 * (below + above) - h

    def mix(j):
        return (h + xx * mu_ref[j:j + 1, :]).astype(bf16)

    r_ref[...] = _bdot(mix(0), wrkv_ref[0])
    k_ref[...] = _bdot(mix(2), wrkv_ref[1])
    v_ref[...] = _bdot(mix(3), wrkv_ref[2])
    g_ref[...] = _bdot(jax.nn.sigmoid(_bdot(mix(5), g1_ref[...])), g2_ref[...])
    xw = mix(1)
    xa = mix(4)
    for d in range(2):
        z = w0_ref[d:d + 1, :] + _bdot(jnp.tanh(_bdot(xw, w1_ref[d])), w2_ref[d])
        softplus = jnp.maximum(-z, 0.0) + jnp.log(1.0 + jnp.exp(-jnp.abs(z)))
        lw_ref[d] = -jnp.exp(-softplus - 0.5)
        a_ref[d] = jax.nn.sigmoid(a0_ref[d:d + 1, :] + _bdot(_bdot(xa, a1_ref[d]), a2_ref[d]))


def _pad_to(w, axis, size):
    pad = [(0, 0)] * w.ndim
    pad[axis] = (0, size - w.shape[axis])
    return jnp.pad(w, pad)


def _rwkv_project(lay, x, m, mu, w_rkv, w0, w1, w2, a0, a1, a2, g1, g2):
    halo = 8
    hpt = ROW_TILE // halo
    last = lay.n // halo - 1
    nd = (lay.n, D)
    consts = [mu, w_rkv.astype(bf16), w0,
              _pad_to(w1, 2, LORA_PAD).astype(bf16), _pad_to(w2, 1, LORA_PAD).astype(bf16), a0,
              _pad_to(a1, 2, LORA_PAD).astype(bf16), _pad_to(a2, 1, LORA_PAD).astype(bf16),
              _pad_to(g1, 1, GATE_PAD).astype(bf16), _pad_to(g2, 0, GATE_PAD).astype(bf16)]
    return pl.pallas_call(
        functools.partial(_rwkv_proj_body, lay),
        grid=(lay.n_tiles,),
        in_specs=[lay.row_spec(),
                  pl.BlockSpec((halo, D), lambda i: (jnp.maximum(i * hpt - 1, 0), 0)),
                  pl.BlockSpec((halo, D), lambda i: (jnp.minimum((i + 1) * hpt, last), 0)),
                  lay.mod_spec(), lay.mod_spec()] + [_const_spec(c.shape) for c in consts],
        out_specs=[lay.row_spec()] * 4 + [pl.BlockSpec((2, ROW_TILE, D), lambda i: (0, i, 0))] * 2,
        out_shape=[jax.ShapeDtypeStruct(nd, f32)] * 4 + [jax.ShapeDtypeStruct((2,) + nd, f32)] * 2,
        compiler_params=_params(1),
        name="rwkv_proj",
    )(x, x, x, m[0], m[1], *consts)


def _split_dot(x, c, parts):
    acc = None
    for _ in range(parts):
        hi = x.astype(bf16)
        t = jnp.dot(hi, c, preferred_element_type=f32)
        acc = t if acc is None else acc + t
        x = x - hi.astype(f32)
    return acc


def _split_dot_left(c, x, parts):
    acc = None
    for _ in range(parts):
        hi = x.astype(bf16)
        t = jnp.dot(c, hi, preferred_element_type=f32)
        acc = t if acc is None else acc + t
        x = x - hi.astype(f32)
    return acc


def _mm(a, b):
    return jnp.dot(a, b, precision=HI, preferred_element_type=f32)


def _mm_nt(a, b):
    return lax.dot_general(a, b, (((1,), (1,)), ((), ())), precision=HI, preferred_element_type=f32)


def _scan_body(lay, rev, r_ref, k_ref, v_ref, lw_ref, a_ref, kk_ref, ka_ref, rk_ref, s0_ref, tri_ref, bd_ref,
               y_ref, bon_ref, sfin_ref, state_ref):
    t_idx = pl.program_id(1)
    i = (lay.n_tiles - 1 - t_idx) if rev else t_idx
    pos, cnt = lay.seq_pos(i)
    first = (pos == cnt - 1) if rev else (pos == 0)
    final = (pos == 0) if rev else (pos == cnt - 1)
    c_len = SCAN_CHUNK
    n_chunks = SCAN_ROWS // c_len

    @pl.when(first)
    def _():
        state_ref[...] = s0_ref[...]

    lane = lax.broadcasted_iota(jnp.int32, (1, LANES), 1)
    head0 = lane < RWKV_HEAD
    rr = lax.broadcasted_iota(jnp.int32, (2 * c_len, 2 * c_len), 0)
    cc = lax.broadcasted_iota(jnp.int32, (2 * c_len, 2 * c_len), 1)
    same = (rr // c_len) == (cc // c_len)
    tt = rr % c_len
    ss = cc % c_len
    strict = same & ((tt < ss) if rev else (tt > ss))
    incl = same & ((tt <= ss) if rev else (tt >= ss))
    tri = tri_ref[...]
    bd = bd_ref[...]
    kk_p = kk_ref[...]
    ka_p = ka_ref[...]
    rk_p = rk_ref[...]

    def stack(xc):
        return jnp.concatenate([jnp.where(head0, xc, 0.0), jnp.where(head0, 0.0, xc)], axis=0)

    def chunk(c, carry):
        ce = (n_chunks - 1 - c) if rev else c
        rows = pl.ds(pl.multiple_of(ce * c_len, c_len), c_len)
        r = r_ref[rows, :]
        k = k_ref[rows, :]
        v = v_ref[rows, :]
        lw = lw_ref[rows, :]
        a = a_ref[rows, :]
        kk = k * kk_p
        nrm = jnp.sqrt(_split_dot(kk * kk, bd, 3))
        kk = kk / jnp.maximum(nrm, 1e-12)
        b = kk * a
        kd = k * (1.0 + (a - 1.0) * ka_p)
        bon_ref[rows, :] = _split_dot(r * kd * rk_p, bd, 3) * v
        cum = _split_dot_left(tri, lw, 3)
        tot = cum[0:1, :] if rev else cum[c_len - 1:c_len, :]
        e_tot = jnp.exp(tot)
        e_neg = jnp.exp(-cum)
        a_t = -kk * jnp.exp(cum - lw)
        r_t = r * jnp.exp(cum)
        b_t = b * e_neg
        k_t = kd * e_neg
        ar = jnp.concatenate([stack(a_t), stack(r_t)], axis=0)
        bk = jnp.concatenate([stack(b_t), stack(k_t)], axis=0)
        m = _mm_nt(ar, bk)
        h2 = 2 * c_len
        n_ab = jnp.where(strict, m[:h2, :h2], 0.0)
        a_ak = jnp.where(strict, m[:h2, h2:], 0.0)
        a_rb = jnp.where(incl, m[h2:, :h2], 0.0)
        a_rk = jnp.where(incl, m[h2:, h2:], 0.0)
        state = state_ref[...]
        x0 = _mm_nt(ar, state)
        v_st = stack(v)
        sa = x0[:h2] + _mm(a_ak, v_st)
        pw = n_ab
        span = 1
        while span < c_len:
            sa = sa + _mm(pw, sa)
            span *= 2
            if span < c_len:
                pw = _mm(pw, pw)
        sav = jnp.concatenate([sa, v_st], axis=0)
        y_st = x0[h2:] + _mm(jnp.concatenate([a_rb, a_rk], axis=1), sav)
        y_ref[rows, :] = y_st[:c_len] + y_st[c_len:]
        bk_hat = bk * e_tot
        state_ref[...] = state * e_tot + _mm(sav.T, bk_hat)
        return carry

    lax.fori_loop(0, n_chunks, chunk, 0)

    @pl.when(final)
    def _():
        sfin_ref[...] = state_ref[...]


def _block_diag_states(s):
    b = s.shape[0]
    s = s.reshape(b, RWKV_HEADS // 2, 2, RWKV_HEAD, RWKV_HEAD)
    eye = jnp.eye(2, dtype=s.dtype)
    out = s[:, :, :, :, None, :] * eye[None, None, :, None, :, None]
    return out.reshape(b, RWKV_HEADS // 2, LANES, LANES)


def _diag_blocks(w):
    b = w.shape[0]
    w = w.reshape(b, RWKV_HEADS // 2, 2, RWKV_HEAD, 2, RWKV_HEAD)
    return jnp.stack([w[:, :, 0, :, 0, :], w[:, :, 1, :, 1, :]], axis=2).reshape(b, RWKV_HEADS, RWKV_HEAD, RWKV_HEAD)


def _rwkv_scan(lay, rev, r, k, v, lw, a, k_k, k_a, r_k, s0):
    d = 1 if rev else 0
    n_seq = lay.bp + lay.bs
    c_len = SCAN_CHUNK
    idx = jnp.arange(c_len)
    tri = ((idx[:, None] <= idx[None, :]) if rev else (idx[:, None] >= idx[None, :])).astype(bf16)
    hh = jnp.arange(LANES) // RWKV_HEAD
    bd = (hh[:, None] == hh[None, :]).astype(bf16)

    def tile(t):
        return (lay.n_tiles - 1 - t) if rev else t

    def seq(t):
        i = tile(t)
        return jnp.where(i < lay.prompt_tiles, i // lay.tiles_per_pseq,
                         lay.bp + (i - lay.prompt_tiles) // lay.tiles_per_sseq)

    row = pl.BlockSpec((SCAN_ROWS, LANES), lambda p, t: (tile(t), p))
    row_d = pl.BlockSpec((None, SCAN_ROWS, LANES), lambda p, t: (d, tile(t), p))
    par = pl.BlockSpec((1, LANES), lambda p, t: (0, p))
    st = pl.BlockSpec((None, None, LANES, LANES), lambda p, t: (seq(t), p, 0, 0))
    return pl.pallas_call(
        functools.partial(_scan_body, lay, rev),
        grid=(RWKV_HEADS // 2, lay.n_tiles),
        in_specs=[row, row, row, row_d, row_d, par, par, par, st,
                  _const_spec((c_len, c_len)), _const_spec((LANES, LANES))],
        out_specs=[row, row, st],
        out_shape=[jax.ShapeDtypeStruct((lay.n, D), f32), jax.ShapeDtypeStruct((lay.n, D), f32),
                   jax.ShapeDtypeStruct((n_seq, RWKV_HEADS // 2, LANES, LANES), f32)],
        scratch_shapes=[pltpu.VMEM((LANES, LANES), f32)],
        compiler_params=_params(2),
        name="rwkv_scan_bwd" if rev else "rwkv_scan_fwd",
    )(r, k, v, lw, a, k_k.reshape(1, D), k_a.reshape(1, D), r_k.reshape(1, D), s0, tri, bd)


def _rwkv_out_body(yf_ref, yb_ref, bf_ref, bb_ref, g_ref, gg_ref, gb_ref, bd_ref, o_ref):
    y = yf_ref[...] + yb_ref[...]
    bd = bd_ref[...]
    inv = 1.0 / RWKV_HEAD
    outs = []
    for c in range(D // LANES):
        yc = y[:, c * LANES:(c + 1) * LANES]
        cen = yc - _split_dot(yc, bd, 3) * inv
        var = _split_dot(cen * cen, bd, 3) * inv
        outs.append(cen * lax.rsqrt(var + GN_EPS))
    yn = jnp.concatenate(outs, axis=1) * gg_ref[...] + gb_ref[...]
    o_ref[...] = ((yn + bf_ref[...] + bb_ref[...]) * g_ref[...]).astype(o_ref.dtype)


def _rwkv(lay, x, m, s0_f, s0_b, mu, w_rkv, w0, w1, w2, a0, a1, a2, g1, g2, k_k, k_a, r_k, gn_g, gn_b):
    r, k, v, g, lw, a = _rwkv_project(lay, x, m, mu, w_rkv, w0, w1, w2, a0, a1, a2, g1, g2)
    zeros = jnp.zeros((lay.bp, RWKV_HEADS // 2, LANES, LANES), f32)
    outs = []
    for rev, s0 in ((False, s0_f), (True, s0_b)):
        s_all = jnp.concatenate([zeros, _block_diag_states(s0.astype(f32))], axis=0)
        outs.append(_rwkv_scan(lay, rev, r, k, v, lw, a, k_k, k_a, r_k, s_all))
    hh = jnp.arange(LANES) // RWKV_HEAD
    bd = (hh[:, None] == hh[None, :]).astype(bf16)
    y = pl.pallas_call(
        _rwkv_out_body,
        grid=(lay.n_tiles,),
        in_specs=[lay.row_spec()] * 5 + [_const_spec((1, D)), _const_spec((1, D)), _const_spec((LANES, LANES))],
        out_specs=lay.row_spec(),
        out_shape=jax.ShapeDtypeStruct((lay.n, D), bf16),
        compiler_params=_params(1),
        name="rwkv_groupnorm_gate",
    )(outs[0][0], outs[1][0], outs[0][1], outs[1][1], g, gn_g.reshape(1, D), gn_b.reshape(1, D), bd)
    return y, _diag_blocks(outs[0][2][:lay.bp]), _diag_blocks(outs[1][2][:lay.bp])


def _router_body(x_ref, sh_ref, sc_ref, w_ref, b_ref, h_ref, route_ref):
    h = x_ref[...] * (1.0 + sc_ref[...]) + sh_ref[...]
    h_ref[...] = h.astype(h_ref.dtype)
    lg = jnp.dot(h, w_ref[...], precision=HI, preferred_element_type=f32) + b_ref[...]
    lane = lax.broadcasted_iota(jnp.int32, lg.shape, 1)

    def top1(vals):
        mx = jnp.max(vals, axis=-1, keepdims=True)
        idx = jnp.min(jnp.where(vals == mx, lane, LANES), axis=-1, keepdims=True)
        return mx, idx

    gl = jnp.where(lane < N_GROUPS, lg, NEG)
    g_max, g_idx = top1(gl)
    p_g = 1.0 / jnp.sum(jnp.exp(gl - g_max), axis=-1, keepdims=True)
    lo = N_GROUPS + g_idx * EXPERTS_PER_GROUP
    el = jnp.where((lane >= lo) & (lane < lo + EXPERTS_PER_GROUP), lg, NEG)
    m1, i1 = top1(el)
    m2, i2 = top1(jnp.where(lane == i1, NEG, el))
    e2 = jnp.exp(m2 - m1)
    w1 = p_g / (1.0 + e2)
    w2 = p_g * e2 / (1.0 + e2)
    route = jnp.where(lane == 0, (i1 - N_GROUPS).astype(f32),
                      jnp.where(lane == 1, (i2 - N_GROUPS).astype(f32),
                                jnp.where(lane == 2, w1, jnp.where(lane == 3, w2, 0.0))))
    route_ref[...] = route


def _expert_body(be_ref, x_ref, wgu_ref, wd_ref, o_ref):
    u = jnp.dot(x_ref[...], wgu_ref[...], preferred_element_type=f32)
    act = _silu(u[:, :EXPERT_FF]) * u[:, EXPERT_FF:]
    o_ref[...] = jnp.dot(act.astype(bf16), wd_ref[...], preferred_element_type=f32)


def _moe_out_body(y_ref0, y_ref1, route_ref, x_ref, gate_ref, g_ref, be_ref, o_ref):
    route = route_ref[...]
    y = route[:, 2:3] * y_ref0[...] + route[:, 3:4] * y_ref1[...]
    z = ALPHA * x_ref[...] + gate_ref[...] * y
    o_ref[...] = _layer_norm(z, g_ref[...], be_ref[...])


def _moe_layer(lay, x, m, w_group, b_group, w_expert, b_expert, w_gu, w_down, ln_g, ln_b):
    n = lay.n
    n_route = N_GROUPS + N_EXPERTS
    w_r = _pad_to(jnp.concatenate([w_group, w_expert], axis=1).astype(f32), 1, LANES)
    b_r = _pad_to(jnp.concatenate([b_group, b_expert]).astype(f32).reshape(1, n_route), 1, LANES)
    h, route = pl.pallas_call(
        _router_body,
        grid=(lay.n_tiles,),
        in_specs=[lay.row_spec(), lay.mod_spec(), lay.mod_spec(), _const_spec((D, LANES)), _const_spec((1, LANES))],
        out_specs=[lay.row_spec(), lay.row_spec(LANES)],
        out_shape=[jax.ShapeDtypeStruct((n, D), bf16), jax.ShapeDtypeStruct((n, LANES), f32)],
        compiler_params=_params(1),
        name="moe_router",
    )(x, m[3], m[4], w_r, b_r)

    s = n * TOP_K
    e_flat = route[:, :TOP_K].astype(jnp.int32).reshape(-1)
    order = jnp.argsort(e_flat).astype(jnp.int32)
    e_sorted = e_flat[order]
    counts = jnp.zeros((N_EXPERTS,), jnp.int32).at[e_flat].add(1)
    padded = (counts + MOE_ROWS - 1) // MOE_ROWS * MOE_ROWS
    pad_end = jnp.cumsum(padded)
    pad_start = pad_end - padded
    start = jnp.cumsum(counts) - counts
    dest_sorted = pad_start[e_sorted] + jnp.arange(s, dtype=jnp.int32) - start[e_sorted]
    n_blocks = -(-s // MOE_ROWS) + N_EXPERTS
    p_rows = n_blocks * MOE_ROWS
    slot_token = jnp.zeros((p_rows,), jnp.int32).at[dest_sorted].set(order // TOP_K)
    block_expert = jnp.minimum(
        jnp.searchsorted(pad_end, jnp.arange(n_blocks, dtype=jnp.int32) * MOE_ROWS, side="right"),
        N_EXPERTS - 1).astype(jnp.int32)
    dest = jnp.zeros((s,), jnp.int32).at[order].set(dest_sorted)

    xb = jnp.take(h, slot_token, axis=0)
    yb = pl.pallas_call(
        _expert_body,
        grid_spec=pltpu.PrefetchScalarGridSpec(
            num_scalar_prefetch=1,
            grid=(n_blocks,),
            in_specs=[pl.BlockSpec((MOE_ROWS, D), lambda i, be: (i, 0)),
                      pl.BlockSpec((None, D, 2 * EXPERT_FF), lambda i, be: (be[i], 0, 0)),
                      pl.BlockSpec((None, EXPERT_FF, D), lambda i, be: (be[i], 0, 0))],
            out_specs=pl.BlockSpec((MOE_ROWS, D), lambda i, be: (i, 0))),
        out_shape=jax.ShapeDtypeStruct((p_rows, D), f32),
        compiler_params=_params(1),
        name="moe_experts",
    )(block_expert, xb, w_gu.astype(bf16), w_down.astype(bf16))
    y2 = jnp.take(yb, dest, axis=0).reshape(n, TOP_K * D)

    return pl.pallas_call(
        _moe_out_body,
        grid=(lay.n_tiles,),
        in_specs=[lay.row_spec(D, 0), lay.row_spec(D, 1), lay.row_spec(LANES), lay.row_spec(), lay.mod_spec(),
                  _const_spec((1, D)), _const_spec((1, D))],
        out_specs=lay.row_spec(),
        out_shape=jax.ShapeDtypeStruct((n, D), f32),
        compiler_params=_params(1),
        name="moe_combine_ln",
    )(y2, y2, route, x, m[5], ln_g.reshape(1, D), ln_b.reshape(1, D))


def kernel(x_prompt, x_sample, c, c_ctx, cache_attn_k, cache_attn_v, state_rwkv_fwd, state_rwkv_bwd, mod_w, mod_b, post_ln_g, post_ln_b, attn_w_qkv, attn_w_o, attn_sink, conv_w_pw1, conv_b_pw1, conv_w_dw, conv_b_dw, conv_ln_g, conv_ln_b, conv_w_pw2, conv_b_pw2, rwkv_mu, rwkv_w_rkv, rwkv_w0, rwkv_w1, rwkv_w2, rwkv_a0, rwkv_a1, rwkv_a2, rwkv_g1, rwkv_g2, rwkv_k_k, rwkv_k_a, rwkv_r_k, rwkv_gn_g, rwkv_gn_b, rwkv_w_o, moe_w_group, moe_b_group, moe_w_expert, moe_b_expert, moe_w_gate_up, moe_w_down):
    bp, sp, _ = x_prompt.shape
    bs, ss, _ = x_sample.shape
    assert 1 + bs <= MOD_ROWS
    lay = _Layout(bp, sp, bs, ss)
    x = jnp.concatenate([x_prompt.reshape(bp * sp, D), x_sample.reshape(bs * ss, D)], axis=0)
    cond = jnp.concatenate([c_ctx[None, :], c, jnp.zeros((MOD_ROWS - 1 - bs, D), f32)], axis=0)
    modt = _modulation_table(cond, mod_w, mod_b)
    zero_b = jnp.zeros((D,), f32)
    new_k, new_v, new_sf, new_sb = [], [], [], []
    n_attn = n_conv = n_rwkv = 0
    for i in range(DEPTH):
        m = modt[i]
        kind = i % 3
        if kind == 0:
            j = n_attn
            qkv = _mod_proj(lay, x, m[0], m[1], attn_w_qkv[j], jnp.zeros((Q_DIM + 2 * KV_DIM,), f32), name="attn_qkv")
            past = cache_attn_k.shape[2]
            o_p = _context_attention(lay, qkv, attn_sink[j])
            o_s = _latent_attention(lay, qkv, cache_attn_k[:, j].reshape(bs, past, KV_DIM),
                                    cache_attn_v[:, j].reshape(bs, past, KV_DIM), attn_sink[j])
            y = jnp.concatenate([o_p, o_s], axis=0)
            w_out, b_out = attn_w_o[j], zero_b
            new_k.append(qkv[:lay.n_prompt, Q_DIM:Q_DIM + KV_DIM].reshape(bp, sp, N_KV_HEADS, HEAD_DIM))
            new_v.append(qkv[:lay.n_prompt, Q_DIM + KV_DIM:].reshape(bp, sp, N_KV_HEADS, HEAD_DIM))
            n_attn += 1
        elif kind == 1:
            j = n_conv
            y = _conformer(lay, x, m, conv_w_pw1[j], conv_b_pw1[j], conv_w_dw[j], conv_b_dw[j], conv_ln_g[j],
                           conv_ln_b[j], conv_w_pw2[j], conv_b_pw2[j])
            w_out, b_out = conv_w_pw2[j], conv_b_pw2[j]
            n_conv += 1
        else:
            j = n_rwkv
            y, sf, sb = _rwkv(lay, x, m, state_rwkv_fwd[:, j], state_rwkv_bwd[:, j], rwkv_mu[j], rwkv_w_rkv[j],
                              rwkv_w0[j], rwkv_w1[j], rwkv_w2[j], rwkv_a0[j], rwkv_a1[j], rwkv_a2[j], rwkv_g1[j],
                              rwkv_g2[j], rwkv_k_k[j], rwkv_k_a[j], rwkv_r_k[j], rwkv_gn_g[j], rwkv_gn_b[j])
            w_out, b_out = rwkv_w_o[j], zero_b
            new_sf.append(sf)
            new_sb.append(sb)
            n_rwkv += 1
        x = _out_proj_ln(lay, y, w_out, b_out, x, m[2], post_ln_g[i, 0], post_ln_b[i, 0], name="mixer_out_ln")
        x = _moe_layer(lay, x, m, moe_w_group[i], moe_b_group[i], moe_w_expert[i], moe_b_expert[i],
                       moe_w_gate_up[i], moe_w_down[i], post_ln_g[i, 1], post_ln_b[i, 1])
    y_prompt = x[:lay.n_prompt].reshape(bp, sp, D)
    y_sample = x[lay.n_prompt:].reshape(bs, ss, D)
    return (y_prompt, y_sample, jnp.stack(new_k, axis=1), jnp.stack(new_v, axis=1),
            jnp.stack(new_sf, axis=1).astype(x_prompt.dtype), jnp.stack(new_sb, axis=1).astype(x_prompt.dtype))
```

```python
import functools

import jax
import jax.numpy as jnp
from jax import lax
from jax.experimental import pallas as pl
from jax.experimental.pallas import tpu as pltpu

f32 = jnp.float32
bf16 = jnp.bfloat16

D = 1024
DEPTH = 4
GRID_W = 64
HEAD_DIM = 64
N_Q_HEADS = D // HEAD_DIM
N_KV_HEADS = N_Q_HEADS // 4
Q_PER_KV = N_Q_HEADS // N_KV_HEADS
Q_DIM = N_Q_HEADS * HEAD_DIM
KV_DIM = N_KV_HEADS * HEAD_DIM
WINDOW = 128
ROPE_BASE = 10000.0
CONV_WIDTH = 31
CONV_PAD = (CONV_WIDTH - 1) // 2
RWKV_HEAD = 64
RWKV_HEADS = D // RWKV_HEAD
GN_EPS = 64e-5
N_GROUPS = 4
EXPERTS_PER_GROUP = 8
N_EXPERTS = N_GROUPS * EXPERTS_PER_GROUP
TOP_K = 2
EXPERT_FF = D // 2
LN_EPS = 1e-5
ALPHA = (2 * DEPTH) ** 0.25

LANES = 128
ROW_TILE = 256
MOE_ROWS = 256
ATT_Q = 128
SCAN_CHUNK = 64
SCAN_ROWS = 256
SCAN_PAIRS = 4
MOD_ROWS = 16
VMEM_LIMIT = 56 * 1024 * 1024
NEG = -1e30
HI = lax.Precision.HIGHEST


def _params(n_axes):
    return pltpu.CompilerParams(dimension_semantics=("arbitrary",) * n_axes,
                                vmem_limit_bytes=VMEM_LIMIT)


def _const_spec(shape):
    nd = len(shape)
    return pl.BlockSpec(shape, lambda *_: (0,) * nd)


class _Layout:
    def __init__(self, bp, sp, bs, ss):
        self.bp, self.sp, self.bs, self.ss = bp, sp, bs, ss
        self.n_prompt = bp * sp
        self.n = bp * sp + bs * ss
        assert sp % ROW_TILE == 0 and ss % ROW_TILE == 0
        self.n_tiles = self.n // ROW_TILE
        self.prompt_tiles = self.n_prompt // ROW_TILE
        self.tiles_per_sseq = ss // ROW_TILE
        self.tiles_per_pseq = sp // ROW_TILE

    def mod_row(self, i):
        return jnp.where(i < self.prompt_tiles, 0, 1 + (i - self.prompt_tiles) // self.tiles_per_sseq)

    def seq_pos(self, i):
        in_p = i < self.prompt_tiles
        pos = jnp.where(in_p, i % self.tiles_per_pseq, (i - self.prompt_tiles) % self.tiles_per_sseq)
        cnt = jnp.where(in_p, self.tiles_per_pseq, self.tiles_per_sseq)
        return pos, cnt

    def row_spec(self, width=D, col=0):
        return pl.BlockSpec((ROW_TILE, width), lambda i: (i, col))

    def mod_spec(self):
        return pl.BlockSpec((None, 1, D), lambda i: (self.mod_row(i), 0, 0))


def _layer_norm(z, g, b, eps=LN_EPS):
    mu = jnp.mean(z, axis=-1, keepdims=True)
    zc = z - mu
    var = jnp.mean(zc * zc, axis=-1, keepdims=True)
    return zc * lax.rsqrt(var + eps) * g + b


def _silu(x):
    return x * jax.nn.sigmoid(x)


def _mod_body(c_ref, w_ref, b_ref, o_ref):
    o_ref[...] = jnp.dot(_silu(c_ref[...]), w_ref[...], precision=HI,
                         preferred_element_type=f32) + b_ref[...]


def _modulation_table(cond, mod_w, mod_b):
    out = pl.pallas_call(
        _mod_body,
        grid=(DEPTH, 6),
        in_specs=[_const_spec((MOD_ROWS, D)),
                  pl.BlockSpec((None, D, D), lambda l, j: (l, 0, j)),
                  pl.BlockSpec((None, 1, D), lambda l, j: (l, 0, j))],
        out_specs=pl.BlockSpec((None, None, MOD_ROWS, D), lambda l, j: (l, j, 0, 0)),
        out_shape=jax.ShapeDtypeStruct((DEPTH, 6, MOD_ROWS, D), f32),
        compiler_params=_params(2),
        name="modulation",
    )(cond, mod_w, mod_b.reshape(DEPTH, 1, 6 * D))
    return out.reshape(DEPTH, 6, MOD_ROWS, 1, D)


def _proj_body(x_ref, sh_ref, sc_ref, w_ref, b_ref, o_ref, *, glu):
    h = x_ref[...] * (1.0 + sc_ref[...]) + sh_ref[...]
    u = jnp.dot(h.astype(bf16), w_ref[...], preferred_element_type=f32) + b_ref[...]
    if glu:
        half = u.shape[1] // 2
        u = u[:, :half] * jax.nn.sigmoid(u[:, half:])
    o_ref[...] = u.astype(o_ref.dtype)


def _mod_proj(lay, x, shift, scale, w, b, *, glu=False, out_dtype=f32, name="proj"):
    k, n = w.shape
    n_out = n // 2 if glu else n
    return pl.pallas_call(
        functools.partial(_proj_body, glu=glu),
        grid=(lay.n_tiles,),
        in_specs=[lay.row_spec(), lay.mod_spec(), lay.mod_spec(),
                  _const_spec((k, n)), _const_spec((1, n))],
        out_specs=lay.row_spec(n_out),
        out_shape=jax.ShapeDtypeStruct((lay.n, n_out), out_dtype),
        compiler_params=_params(1),
        name=name,
    )(x, shift, scale, w.astype(bf16), b.reshape(1, n))


def _out_ln_body(y_ref, w_ref, b_ref, x_ref, gate_ref, g_ref, be_ref, o_ref):
    t = jnp.dot(y_ref[...].astype(bf16), w_ref[...], preferred_element_type=f32) + b_ref[...]
    z = ALPHA * x_ref[...] + gate_ref[...] * t
    o_ref[...] = _layer_norm(z, g_ref[...], be_ref[...])


def _out_proj_ln(lay, y, w, b, x, gate, ln_g, ln_b, name="out_ln"):
    k = w.shape[0]
    return pl.pallas_call(
        _out_ln_body,
        grid=(lay.n_tiles,),
        in_specs=[lay.row_spec(k), _const_spec((k, D)), _const_spec((1, D)),
                  lay.row_spec(), lay.mod_spec(), _const_spec((1, D)), _const_spec((1, D))],
        out_specs=lay.row_spec(),
        out_shape=jax.ShapeDtypeStruct((lay.n, D), f32),
        compiler_params=_params(1),
        name=name,
    )(y, w.astype(bf16), b.reshape(1, D), x, gate, ln_g.reshape(1, D), ln_b.reshape(1, D))


def _sink_column(sink_ref, h, rows_per_head):
    r = lax.broadcasted_iota(jnp.int32, (Q_PER_KV * rows_per_head, 1), 0) // rows_per_head
    col = jnp.full(r.shape, sink_ref[h * Q_PER_KV], f32)
    for g in range(1, Q_PER_KV):
        col = jnp.where(r == g, sink_ref[h * Q_PER_KV + g], col)
    return col


def _stack_heads(q, h):
    return jnp.concatenate(
        [q[:, (h * Q_PER_KV + g) * HEAD_DIM:(h * Q_PER_KV + g + 1) * HEAD_DIM] for g in range(Q_PER_KV)], axis=0)


def _nt(a, b):
    return lax.dot_general(a, b, (((1,), (1,)), ((), ())), preferred_element_type=f32)


def _ctx_attn_body(sink_ref, q_ref, k_ref, v_ref, o_ref):
    rows = q_ref.shape[0]
    q = q_ref[...] * (HEAD_DIM ** -0.5)
    k = k_ref[...]
    v = v_ref[...]
    outs = [None] * N_Q_HEADS
    for h in range(N_KV_HEADS):
        qh = _stack_heads(q, h).astype(bf16)
        kh = k[:, h * HEAD_DIM:(h + 1) * HEAD_DIM].astype(bf16)
        vh = v[:, h * HEAD_DIM:(h + 1) * HEAD_DIM].astype(bf16)
        s = _nt(qh, kh)
        sk = _sink_column(sink_ref, h, rows)
        m = jnp.maximum(jnp.max(s, axis=-1, keepdims=True), sk)
        p = jnp.exp(s - m)
        den = jnp.sum(p, axis=-1, keepdims=True) + jnp.exp(sk - m)
        o = jnp.dot(p.astype(bf16), vh, preferred_element_type=f32) / den
        for g in range(Q_PER_KV):
            outs[h * Q_PER_KV + g] = o[g * rows:(g + 1) * rows]
    o_ref[...] = jnp.concatenate(outs, axis=1).astype(o_ref.dtype)


def _context_attention(lay, qkv, sink):
    sp = lay.sp
    kcol = Q_DIM // KV_DIM
    return pl.pallas_call(
        _ctx_attn_body,
        grid=(lay.bp,),
        in_specs=[pl.BlockSpec(memory_space=pltpu.SMEM),
                  pl.BlockSpec((sp, Q_DIM), lambda b: (b, 0)),
                  pl.BlockSpec((sp, KV_DIM), lambda b: (b, kcol)),
                  pl.BlockSpec((sp, KV_DIM), lambda b: (b, kcol + 1))],
        out_specs=pl.BlockSpec((sp, Q_DIM), lambda b: (b, 0)),
        out_shape=jax.ShapeDtypeStruct((lay.n_prompt, Q_DIM), bf16),
        compiler_params=_params(1),
        name="ctx_attn",
    )(sink, qkv, qkv, qkv)


def _rope_tables(t_len):
    t = jnp.arange(t_len)
    row = (t // GRID_W).astype(f32)
    col = (t % GRID_W).astype(f32)
    quarter = HEAD_DIM // 4
    d = jnp.arange(LANES) % HEAD_DIM
    inv = ROPE_BASE ** (-(d % quarter).astype(f32) / quarter)
    pos = jnp.where((d < HEAD_DIM // 2)[None, :], row[:, None], col[:, None])
    ang = pos * inv[None, :]
    cs, sn = jnp.cos(ang), jnp.sin(ang)
    first = ((d % (2 * quarter)) < quarter)[None, :]
    return cs, jnp.where(first, -sn, 0.0), jnp.where(first, 0.0, sn)


def _rope(x, cs, sa, sb):
    quarter = HEAD_DIM // 4
    outs = []
    for c in range(x.shape[1] // LANES):
        xc = x[:, c * LANES:(c + 1) * LANES]
        outs.append(xc * cs + pltpu.roll(xc, LANES - quarter, axis=1) * sa + pltpu.roll(xc, quarter, axis=1) * sb)
    return outs[0] if len(outs) == 1 else jnp.concatenate(outs, axis=1)


def _lat_attn_body(sink_ref, q_ref, k_ref, v_ref, kc_ref, vc_ref, cs_ref, sa_ref, sb_ref, o_ref):
    j = pl.program_id(1)
    t_len = k_ref.shape[0]
    win = 3 * ATT_Q
    start = pl.multiple_of(jnp.clip((j - 1) * ATT_Q, 0, t_len - win), ATT_Q)
    q0 = pl.multiple_of(j * ATT_Q, ATT_Q)
    q = _rope(q_ref[...], cs_ref[pl.ds(q0, ATT_Q), :], sa_ref[pl.ds(q0, ATT_Q), :],
              sb_ref[pl.ds(q0, ATT_Q), :]) * (HEAD_DIM ** -0.5)
    kw = _rope(k_ref[pl.ds(start, win), :], cs_ref[pl.ds(start, win), :], sa_ref[pl.ds(start, win), :],
               sb_ref[pl.ds(start, win), :])
    vw = v_ref[pl.ds(start, win), :]
    kc = kc_ref[...]
    vc = vc_ref[...]
    rows = Q_PER_KV * ATT_Q
    qpos = q0 + lax.broadcasted_iota(jnp.int32, (rows, win), 0) % ATT_Q
    kpos = start + lax.broadcasted_iota(jnp.int32, (rows, win), 1)
    valid = jnp.abs(qpos - kpos) <= WINDOW
    outs = [None] * N_Q_HEADS
    for h in range(N_KV_HEADS):
        sl = slice(h * HEAD_DIM, (h + 1) * HEAD_DIM)
        qh = _stack_heads(q, h).astype(bf16)
        s_c = _nt(qh, kc[:, sl].astype(bf16))
        s_l = jnp.where(valid, _nt(qh, kw[:, sl].astype(bf16)), NEG)
        sk = _sink_column(sink_ref, h, ATT_Q)
        m = jnp.maximum(jnp.maximum(jnp.max(s_c, axis=-1, keepdims=True),
                                    jnp.max(s_l, axis=-1, keepdims=True)), sk)
        p_c = jnp.exp(s_c - m)
        p_l = jnp.exp(s_l - m)
        den = jnp.sum(p_c, axis=-1, keepdims=True) + jnp.sum(p_l, axis=-1, keepdims=True) + jnp.exp(sk - m)
        o = (jnp.dot(p_c.astype(bf16), vc[:, sl].astype(bf16), preferred_element_type=f32)
             + jnp.dot(p_l.astype(bf16), vw[:, sl].astype(bf16), preferred_element_type=f32)) / den
        for g in range(Q_PER_KV):
            outs[h * Q_PER_KV + g] = o[g * ATT_Q:(g + 1) * ATT_Q]
    o_ref[...] = jnp.concatenate(outs, axis=1).astype(o_ref.dtype)


def _latent_attention(lay, qkv, kc, vc, sink):
    ss = lay.ss
    past = kc.shape[1]
    nq = ss // ATT_Q
    q_base = lay.n_prompt // ATT_Q
    s_base = lay.n_prompt // ss
    assert lay.n_prompt % ss == 0
    kcol = Q_DIM // KV_DIM
    cs, sa, sb = _rope_tables(ss)
    return pl.pallas_call(
        _lat_attn_body,
        grid=(lay.bs, nq),
        in_specs=[pl.BlockSpec(memory_space=pltpu.SMEM),
                  pl.BlockSpec((ATT_Q, Q_DIM), lambda b, j: (q_base + b * nq + j, 0)),
                  pl.BlockSpec((ss, KV_DIM), lambda b, j: (s_base + b, kcol)),
                  pl.BlockSpec((ss, KV_DIM), lambda b, j: (s_base + b, kcol + 1)),
                  pl.BlockSpec((None, past, KV_DIM), lambda b, j: (b, 0, 0)),
                  pl.BlockSpec((None, past, KV_DIM), lambda b, j: (b, 0, 0)),
                  _const_spec((ss, LANES)), _const_spec((ss, LANES)), _const_spec((ss, LANES))],
        out_specs=pl.BlockSpec((ATT_Q, Q_DIM), lambda b, j: (b * nq + j, 0)),
        out_shape=jax.ShapeDtypeStruct((lay.bs * ss, Q_DIM), bf16),
        compiler_params=_params(2),
        name="lat_attn",
    )(sink, qkv, qkv, qkv, kc, vc, cs, sa, sb)


CONV_HALO = 16
CONV_SUB = 32


def _dwconv_body(lay, u_ref, prev_ref, next_ref, w_ref, b_ref, g_ref, be_ref, o_ref, buf_ref):
    i = pl.program_id(0)
    pos, cnt = lay.seq_pos(i)
    buf_ref[pl.ds(0, CONV_HALO), :] = jnp.where(pos > 0, prev_ref[...], 0.0)
    buf_ref[pl.ds(CONV_HALO, ROW_TILE), :] = u_ref[...]
    buf_ref[pl.ds(CONV_HALO + ROW_TILE, CONV_HALO), :] = jnp.where(pos < cnt - 1, next_ref[...], 0.0)
    w = w_ref[...]
    off = CONV_HALO - CONV_PAD

    for s in range(ROW_TILE // CONV_SUB):
        r0 = s * CONV_SUB
        acc = jnp.zeros((CONV_SUB, D), f32)
        for t in range(CONV_WIDTH):
            acc = acc + buf_ref[pl.ds(r0 + off + t, CONV_SUB), :] * w[t:t + 1, :]
        z = _layer_norm(acc + b_ref[...], g_ref[...], be_ref[...])
        o_ref[pl.ds(r0, CONV_SUB), :] = _silu(z).astype(o_ref.dtype)


def _dwconv_ln_swish(lay, u, w_dw, b_dw, ln_g, ln_b):
    hpt = ROW_TILE // CONV_HALO
    last = lay.n // CONV_HALO - 1
    return pl.pallas_call(
        functools.partial(_dwconv_body, lay),
        grid=(lay.n_tiles,),
        in_specs=[lay.row_spec(),
                  pl.BlockSpec((CONV_HALO, D), lambda i: (jnp.maximum(i * hpt - 1, 0), 0)),
                  pl.BlockSpec((CONV_HALO, D), lambda i: (jnp.minimum((i + 1) * hpt, last), 0)),
                  _const_spec((CONV_WIDTH, D)), _const_spec((1, D)), _const_spec((1, D)), _const_spec((1, D))],
        out_specs=lay.row_spec(),
        out_shape=jax.ShapeDtypeStruct((lay.n, D), bf16),
        scratch_shapes=[pltpu.VMEM((ROW_TILE + 2 * CONV_HALO, D), f32)],
        compiler_params=_params(1),
        name="dwconv",
    )(u, u, u, w_dw, b_dw.reshape(1, D), ln_g.reshape(1, D), ln_b.reshape(1, D))


def _conformer(lay, x, m, w_pw1, b_pw1, w_dw, b_dw, ln_g, ln_b, w_pw2, b_pw2):
    u = _mod_proj(lay, x, m[0], m[1], w_pw1, b_pw1, glu=True, name="conv_pw1_glu")
    return _dwconv_ln_swish(lay, u, w_dw, b_dw, ln_g, ln_b)


LORA_PAD = 128
GATE_PAD = 256


def _bdot(a, w):
    return jnp.dot(a.astype(bf16), w, preferred_element_type=f32)


def _rwkv_proj_body(lay, x_ref, prev_ref, next_ref, sh_ref, sc_ref, mu_ref, wrkv_ref, w0_ref, w1_ref, w2_ref,
                    a0_ref, a1_ref, a2_ref, g1_ref, g2_ref, r_ref, k_ref, v_ref, g_ref, lw_ref, a_ref):
    i = pl.program_id(0)
    pos, cnt = lay.seq_pos(i)
    sc = 1.0 + sc_ref[...]
    sh = sh_ref[...]
    h = x_ref[...] * sc + sh
    hb = prev_ref.shape[0]
    h_prev = jnp.where(pos > 0, prev_ref[hb - 1:hb, :] * sc + sh, 0.0)
    h_next = jnp.where(pos < cnt - 1, next_ref[0:1, :] * sc + sh, 0.0)
    row = lax.broadcasted_iota(jnp.int32, (ROW_TILE, 1), 0)
    below = jnp.where(row == 0, h_prev, pltpu.roll(h, 1, axis=0))
    above = jnp.where(row == ROW_TILE - 1, h_next, pltpu.roll(h, ROW_TILE - 1, axis=0))
    xx = 0.5 * (below + above) - h

    def mix(j):
        return (h + xx * mu_ref[j:j + 1, :]).astype(bf16)

    r_ref[...] = _bdot(mix(0), wrkv_ref[0])
    k_ref[...] = _bdot(mix(2), wrkv_ref[1])
    v_ref[...] = _bdot(mix(3), wrkv_ref[2])
    g_ref[...] = _bdot(jax.nn.sigmoid(_bdot(mix(5), g1_ref[...])), g2_ref[...])
    xw = mix(1)
    xa = mix(4)
    for d in range(2):
        z = w0_ref[d:d + 1, :] + _bdot(jnp.tanh(_bdot(xw, w1_ref[d])), w2_ref[d])
        softplus = jnp.maximum(-z, 0.0) + jnp.log(1.0 + jnp.exp(-jnp.abs(z)))
        lw_ref[d] = -jnp.exp(-softplus - 0.5)
        a_ref[d] = jax.nn.sigmoid(a0_ref[d:d + 1, :] + _bdot(_bdot(xa, a1_ref[d]), a2_ref[d]))


def _pad_to(w, axis, size):
    pad = [(0, 0)] * w.ndim
    pad[axis] = (0, size - w.shape[axis])
    return jnp.pad(w, pad)


def _rwkv_project(lay, x, m, mu, w_rkv, w0, w1, w2, a0, a1, a2, g1, g2):
    halo = 8
    hpt = ROW_TILE // halo
    last = lay.n // halo - 1
    nd = (lay.n, D)
    consts = [mu, w_rkv.astype(bf16), w0,
              _pad_to(w1, 2, LORA_PAD).astype(bf16), _pad_to(w2, 1, LORA_PAD).astype(bf16), a0,
              _pad_to(a1, 2, LORA_PAD).astype(bf16), _pad_to(a2, 1, LORA_PAD).astype(bf16),
              _pad_to(g1, 1, GATE_PAD).astype(bf16), _pad_to(g2, 0, GATE_PAD).astype(bf16)]
    return pl.pallas_call(
        functools.partial(_rwkv_proj_body, lay),
        grid=(lay.n_tiles,),
        in_specs=[lay.row_spec(),
                  pl.BlockSpec((halo, D), lambda i: (jnp.maximum(i * hpt - 1, 0), 0)),
                  pl.BlockSpec((halo, D), lambda i: (jnp.minimum((i + 1) * hpt, last), 0)),
                  lay.mod_spec(), lay.mod_spec()] + [_const_spec(c.shape) for c in consts],
        out_specs=[lay.row_spec()] * 4 + [pl.BlockSpec((2, ROW_TILE, D), lambda i: (0, i, 0))] * 2,
        out_shape=[jax.ShapeDtypeStruct(nd, f32)] * 4 + [jax.ShapeDtypeStruct((2,) + nd, f32)] * 2,
        compiler_params=_params(1),
        name="rwkv_proj",
    )(x, x, x, m[0], m[1], *consts)


def _split_dot(x, c, parts):
    acc = None
    for _ in range(parts):
        hi = x.astype(bf16)
        t = jnp.dot(hi, c, preferred_element_type=f32)
        acc = t if acc is None else acc + t
        x = x - hi.astype(f32)
    return acc


def _split_dot_left(c, x, parts):
    acc = None
    for _ in range(parts):
        hi = x.astype(bf16)
        t = jnp.dot(c, hi, preferred_element_type=f32)
        acc = t if acc is None else acc + t
        x = x - hi.astype(f32)
    return acc


def _mm(a, b):
    return jnp.dot(a.astype(bf16), b.astype(bf16), preferred_element_type=f32)


def _mmh(a, b):
    return jnp.dot(a, b, precision=HI, preferred_element_type=f32)


def _mm_nt(a, b):
    return lax.dot_general(a.astype(bf16), b.astype(bf16), (((1,), (1,)), ((), ())), preferred_element_type=f32)


def _scan_body(lay, rev, r_ref, k_ref, v_ref, lw_ref, a_ref, kk_ref, ka_ref, rk_ref, s0_ref, tri_ref, bd_ref,
               y_ref, bon_ref, sfin_ref, state_ref):
    t_idx = pl.program_id(1)
    i = (lay.n_tiles - 1 - t_idx) if rev else t_idx
    pos, cnt = lay.seq_pos(i)
    first = (pos == cnt - 1) if rev else (pos == 0)
    final = (pos == 0) if rev else (pos == cnt - 1)
    c_len = SCAN_CHUNK
    n_chunks = SCAN_ROWS // c_len

    @pl.when(first)
    def _():
        state_ref[...] = s0_ref[...]

    lane = lax.broadcasted_iota(jnp.int32, (1, LANES), 1)
    head0 = lane < RWKV_HEAD
    rr = lax.broadcasted_iota(jnp.int32, (2 * c_len, 2 * c_len), 0)
    cc = lax.broadcasted_iota(jnp.int32, (2 * c_len, 2 * c_len), 1)
    same = (rr // c_len) == (cc // c_len)
    tt = rr % c_len
    ss = cc % c_len
    strict = same & ((tt < ss) if rev else (tt > ss))
    incl = same & ((tt <= ss) if rev else (tt >= ss))
    tri = tri_ref[...]
    bd = bd_ref[...]

    def stack(xc):
        return jnp.concatenate([jnp.where(head0, xc, 0.0), jnp.where(head0, 0.0, xc)], axis=0)

    def pair_chunk(rows, p):
        cols = slice(p * LANES, (p + 1) * LANES)
        r = r_ref[rows, cols]
        k = k_ref[rows, cols]
        v = v_ref[rows, cols]
        lw = lw_ref[rows, cols]
        a = a_ref[rows, cols]
        kk = k * kk_ref[:, cols]
        nrm = jnp.sqrt(_split_dot(kk * kk, bd, 3))
        kk = kk / jnp.maximum(nrm, 1e-12)
        b = kk * a
        kd = k * (1.0 + (a - 1.0) * ka_ref[:, cols])
        bon_ref[rows, cols] = _split_dot(r * kd * rk_ref[:, cols], bd, 3) * v
        cum = _split_dot_left(tri, lw, 3)
        tot = cum[0:1, :] if rev else cum[c_len - 1:c_len, :]
        e_tot = jnp.exp(tot)
        e_neg = jnp.exp(-cum)
        a_t = -kk * jnp.exp(cum - lw)
        r_t = r * jnp.exp(cum)
        b_t = b * e_neg
        k_t = kd * e_neg
        ar = jnp.concatenate([stack(a_t), stack(r_t)], axis=0)
        bk = jnp.concatenate([stack(b_t), stack(k_t)], axis=0)
        m = _mm_nt(ar, bk)
        h2 = 2 * c_len
        n_ab = jnp.where(strict, m[:h2, :h2], 0.0)
        a_ak = jnp.where(strict, m[:h2, h2:], 0.0)
        a_rb = jnp.where(incl, m[h2:, :h2], 0.0)
        a_rk = jnp.where(incl, m[h2:, h2:], 0.0)
        state = state_ref[p]
        x0 = _mm_nt(ar, state)
        v_st = stack(v)
        sa = x0[:h2] + _mm(a_ak, v_st)
        pw = n_ab
        span = 1
        while span < c_len:
            sa = sa + _mmh(pw, sa)
            span *= 2
            if span < c_len:
                pw = _mmh(pw, pw)
        sav = jnp.concatenate([sa, v_st], axis=0)
        y_st = x0[h2:] + _mm(jnp.concatenate([a_rb, a_rk], axis=1), sav)
        y_ref[rows, cols] = y_st[:c_len] + y_st[c_len:]
        state_ref[p] = state * e_tot + _mm(sav.T, bk * e_tot)

    def chunk(c, carry):
        ce = (n_chunks - 1 - c) if rev else c
        rows = pl.ds(pl.multiple_of(ce * c_len, c_len), c_len)
        for p in range(SCAN_PAIRS):
            pair_chunk(rows, p)
        return carry

    lax.fori_loop(0, n_chunks, chunk, 0)

    @pl.when(final)
    def _():
        sfin_ref[...] = state_ref[...]


def _block_diag_states(s):
    b = s.shape[0]
    s = s.reshape(b, RWKV_HEADS // 2, 2, RWKV_HEAD, RWKV_HEAD)
    eye = jnp.eye(2, dtype=s.dtype)
    out = s[:, :, :, :, None, :] * eye[None, None, :, None, :, None]
    return out.reshape(b, RWKV_HEADS // 2, LANES, LANES)


def _diag_blocks(w):
    b = w.shape[0]
    w = w.reshape(b, RWKV_HEADS // 2, 2, RWKV_HEAD, 2, RWKV_HEAD)
    return jnp.stack([w[:, :, 0, :, 0, :], w[:, :, 1, :, 1, :]], axis=2).reshape(b, RWKV_HEADS, RWKV_HEAD, RWKV_HEAD)


def _rwkv_scan(lay, rev, r, k, v, lw, a, k_k, k_a, r_k, s0):
    d = 1 if rev else 0
    n_seq = lay.bp + lay.bs
    c_len = SCAN_CHUNK
    idx = jnp.arange(c_len)
    tri = ((idx[:, None] <= idx[None, :]) if rev else (idx[:, None] >= idx[None, :])).astype(bf16)
    hh = jnp.arange(LANES) // RWKV_HEAD
    bd = (hh[:, None] == hh[None, :]).astype(bf16)

    def tile(t):
        return (lay.n_tiles - 1 - t) if rev else t

    def seq(t):
        i = tile(t)
        return jnp.where(i < lay.prompt_tiles, i // lay.tiles_per_pseq,
                         lay.bp + (i - lay.prompt_tiles) // lay.tiles_per_sseq)

    width = SCAN_PAIRS * LANES
    row = pl.BlockSpec((SCAN_ROWS, width), lambda p, t: (tile(t), p))
    row_d = pl.BlockSpec((None, SCAN_ROWS, width), lambda p, t: (d, tile(t), p))
    par = pl.BlockSpec((1, width), lambda p, t: (0, p))
    st = pl.BlockSpec((None, SCAN_PAIRS, LANES, LANES), lambda p, t: (seq(t), p, 0, 0))
    return pl.pallas_call(
        functools.partial(_scan_body, lay, rev),
        grid=(RWKV_HEADS // 2 // SCAN_PAIRS, lay.n_tiles),
        in_specs=[row, row, row, row_d, row_d, par, par, par, st,
                  _const_spec((c_len, c_len)), _const_spec((LANES, LANES))],
        out_specs=[row, row, st],
        out_shape=[jax.ShapeDtypeStruct((lay.n, D), f32), jax.ShapeDtypeStruct((lay.n, D), f32),
                   jax.ShapeDtypeStruct((n_seq, RWKV_HEADS // 2, LANES, LANES), f32)],
        scratch_shapes=[pltpu.VMEM((SCAN_PAIRS, LANES, LANES), f32)],
        compiler_params=_params(2),
        name="rwkv_scan_bwd" if rev else "rwkv_scan_fwd",
    )(r, k, v, lw, a, k_k.reshape(1, D), k_a.reshape(1, D), r_k.reshape(1, D), s0, tri, bd)


def _rwkv_out_body(yf_ref, yb_ref, bf_ref, bb_ref, g_ref, gg_ref, gb_ref, bd_ref, o_ref):
    y = yf_ref[...] + yb_ref[...]
    bd = bd_ref[...]
    inv = 1.0 / RWKV_HEAD
    outs = []
    for c in range(D // LANES):
        yc = y[:, c * LANES:(c + 1) * LANES]
        cen = yc - _split_dot(yc, bd, 3) * inv
        var = _split_dot(cen * cen, bd, 3) * inv
        outs.append(cen * lax.rsqrt(var + GN_EPS))
    yn = jnp.concatenate(outs, axis=1) * gg_ref[...] + gb_ref[...]
    o_ref[...] = ((yn + bf_ref[...] + bb_ref[...]) * g_ref[...]).astype(o_ref.dtype)


def _rwkv(lay, x, m, s0_f, s0_b, mu, w_rkv, w0, w1, w2, a0, a1, a2, g1, g2, k_k, k_a, r_k, gn_g, gn_b):
    r, k, v, g, lw, a = _rwkv_project(lay, x, m, mu, w_rkv, w0, w1, w2, a0, a1, a2, g1, g2)
    zeros = jnp.zeros((lay.bp, RWKV_HEADS // 2, LANES, LANES), f32)
    outs = []
    for rev, s0 in ((False, s0_f), (True, s0_b)):
        s_all = jnp.concatenate([zeros, _block_diag_states(s0.astype(f32))], axis=0)
        outs.append(_rwkv_scan(lay, rev, r, k, v, lw, a, k_k, k_a, r_k, s_all))
    hh = jnp.arange(LANES) // RWKV_HEAD
    bd = (hh[:, None] == hh[None, :]).astype(bf16)
    y = pl.pallas_call(
        _rwkv_out_body,
        grid=(lay.n_tiles,),
        in_specs=[lay.row_spec()] * 5 + [_const_spec((1, D)), _const_spec((1, D)), _const_spec((LANES, LANES))],
        out_specs=lay.row_spec(),
        out_shape=jax.ShapeDtypeStruct((lay.n, D), bf16),
        compiler_params=_params(1),
        name="rwkv_groupnorm_gate",
    )(outs[0][0], outs[1][0], outs[0][1], outs[1][1], g, gn_g.reshape(1, D), gn_b.reshape(1, D), bd)
    return y, _diag_blocks(outs[0][2][:lay.bp]), _diag_blocks(outs[1][2][:lay.bp])


def _router_body(x_ref, sh_ref, sc_ref, w_ref, b_ref, h_ref, route_ref):
    h = x_ref[...] * (1.0 + sc_ref[...]) + sh_ref[...]
    h_ref[...] = h.astype(h_ref.dtype)
    lg = jnp.dot(h, w_ref[...], precision=HI, preferred_element_type=f32) + b_ref[...]
    lane = lax.broadcasted_iota(jnp.int32, lg.shape, 1)

    def top1(vals):
        mx = jnp.max(vals, axis=-1, keepdims=True)
        idx = jnp.min(jnp.where(vals == mx, lane, LANES), axis=-1, keepdims=True)
        return mx, idx

    gl = jnp.where(lane < N_GROUPS, lg, NEG)
    g_max, g_idx = top1(gl)
    p_g = 1.0 / jnp.sum(jnp.exp(gl - g_max), axis=-1, keepdims=True)
    lo = N_GROUPS + g_idx * EXPERTS_PER_GROUP
    el = jnp.where((lane >= lo) & (lane < lo + EXPERTS_PER_GROUP), lg, NEG)
    m1, i1 = top1(el)
    m2, i2 = top1(jnp.where(lane == i1, NEG, el))
    e2 = jnp.exp(m2 - m1)
    w1 = p_g / (1.0 + e2)
    w2 = p_g * e2 / (1.0 + e2)
    route = jnp.where(lane == 0, (i1 - N_GROUPS).astype(f32),
                      jnp.where(lane == 1, (i2 - N_GROUPS).astype(f32),
                                jnp.where(lane == 2, w1, jnp.where(lane == 3, w2, 0.0))))
    route_ref[...] = route


def _expert_body(be_ref, nu_ref, x_ref, wgu_ref, wd_ref, o_ref, wgu_s, wd_s):
    i = pl.program_id(0)
    used = i < nu_ref[0]

    @pl.when(used & ((i == 0) | (be_ref[i] != be_ref[jnp.maximum(i - 1, 0)])))
    def _():
        wgu_s[...] = wgu_ref[...].astype(bf16)
        wd_s[...] = wd_ref[...].astype(bf16)

    @pl.when(used)
    def _():
        u = jnp.dot(x_ref[...], wgu_s[...], preferred_element_type=f32)
        act = _silu(u[:, :EXPERT_FF]) * u[:, EXPERT_FF:]
        o_ref[...] = jnp.dot(act.astype(bf16), wd_s[...], preferred_element_type=f32)

    @pl.when(jnp.logical_not(used))
    def _():
        o_ref[...] = jnp.zeros_like(o_ref)


def _moe_out_body(y_ref0, y_ref1, route_ref, x_ref, gate_ref, g_ref, be_ref, o_ref):
    route = route_ref[...]
    y = route[:, 2:3] * y_ref0[...] + route[:, 3:4] * y_ref1[...]
    z = ALPHA * x_ref[...] + gate_ref[...] * y
    o_ref[...] = _layer_norm(z, g_ref[...], be_ref[...])


def _moe_layer(lay, x, m, w_group, b_group, w_expert, b_expert, w_gu, w_down, ln_g, ln_b):
    n = lay.n
    n_route = N_GROUPS + N_EXPERTS
    w_r = _pad_to(jnp.concatenate([w_group, w_expert], axis=1).astype(f32), 1, LANES)
    b_r = _pad_to(jnp.concatenate([b_group, b_expert]).astype(f32).reshape(1, n_route), 1, LANES)
    h, route = pl.pallas_call(
        _router_body,
        grid=(lay.n_tiles,),
        in_specs=[lay.row_spec(), lay.mod_spec(), lay.mod_spec(), _const_spec((D, LANES)), _const_spec((1, LANES))],
        out_specs=[lay.row_spec(), lay.row_spec(LANES)],
        out_shape=[jax.ShapeDtypeStruct((n, D), bf16), jax.ShapeDtypeStruct((n, LANES), f32)],
        compiler_params=_params(1),
        name="moe_router",
    )(x, m[3], m[4], w_r, b_r)

    s = n * TOP_K
    e_flat = route[:, :TOP_K].astype(jnp.int32).reshape(-1)
    pair = jnp.arange(s, dtype=jnp.int32)
    e_sorted, order = lax.sort((e_flat, pair), num_keys=1, is_stable=True)
    experts = jnp.arange(N_EXPERTS, dtype=jnp.int32)
    counts = jnp.sum((e_flat[:, None] == experts[None, :]).astype(jnp.int32), axis=0)
    padded = (counts + MOE_ROWS - 1) // MOE_ROWS * MOE_ROWS
    pad_end = jnp.cumsum(padded)
    pad_start = pad_end - padded
    start = jnp.cumsum(counts) - counts
    dest_sorted = pad_start[e_sorted] + pair - start[e_sorted]
    n_blocks = -(-s // MOE_ROWS) + N_EXPERTS
    p_rows = n_blocks * MOE_ROWS
    block_start = jnp.arange(n_blocks, dtype=jnp.int32) * MOE_ROWS
    block_expert = jnp.minimum(jnp.sum((pad_end[None, :] <= block_start[:, None]).astype(jnp.int32), axis=1),
                               N_EXPERTS - 1)
    n_used = (pad_end[N_EXPERTS - 1] // MOE_ROWS).astype(jnp.int32).reshape(1)
    slot = jnp.arange(p_rows, dtype=jnp.int32)
    slot_expert = jnp.repeat(block_expert, MOE_ROWS)
    in_run = jnp.minimum(slot - pad_start[slot_expert], jnp.maximum(counts[slot_expert] - 1, 0))
    slot_token = order[jnp.clip(start[slot_expert] + in_run, 0, s - 1)] // TOP_K
    _, dest = lax.sort((order, dest_sorted), num_keys=1)
    dest = dest.reshape(n, TOP_K)

    xb = jnp.take(h, slot_token, axis=0)
    yb = pl.pallas_call(
        _expert_body,
        grid_spec=pltpu.PrefetchScalarGridSpec(
            num_scalar_prefetch=2,
            grid=(n_blocks,),
            in_specs=[pl.BlockSpec((MOE_ROWS, D), lambda i, be, nu: (jnp.minimum(i, nu[0] - 1), 0)),
                      pl.BlockSpec((None, D, 2 * EXPERT_FF), lambda i, be, nu: (be[i], 0, 0)),
                      pl.BlockSpec((None, EXPERT_FF, D), lambda i, be, nu: (be[i], 0, 0))],
            out_specs=pl.BlockSpec((MOE_ROWS, D), lambda i, be, nu: (i, 0)),
            scratch_shapes=[pltpu.VMEM((D, 2 * EXPERT_FF), bf16), pltpu.VMEM((EXPERT_FF, D), bf16)]),
        out_shape=jax.ShapeDtypeStruct((p_rows, D), f32),
        compiler_params=_params(1),
        name="moe_experts",
    )(block_expert, n_used, xb, w_gu, w_down)
    y0 = jnp.take(yb, dest[:, 0], axis=0)
    y1 = jnp.take(yb, dest[:, 1], axis=0)

    return pl.pallas_call(
        _moe_out_body,
        grid=(lay.n_tiles,),
        in_specs=[lay.row_spec(), lay.row_spec(), lay.row_spec(LANES), lay.row_spec(), lay.mod_spec(),
                  _const_spec((1, D)), _const_spec((1, D))],
        out_specs=lay.row_spec(),
        out_shape=jax.ShapeDtypeStruct((n, D), f32),
        compiler_params=_params(1),
        name="moe_combine_ln",
    )(y0, y1, route, x, m[5], ln_g.reshape(1, D), ln_b.reshape(1, D))


def kernel(x_prompt, x_sample, c, c_ctx, cache_attn_k, cache_attn_v, state_rwkv_fwd, state_rwkv_bwd, mod_w, mod_b, post_ln_g, post_ln_b, attn_w_qkv, attn_w_o, attn_sink, conv_w_pw1, conv_b_pw1, conv_w_dw, conv_b_dw, conv_ln_g, conv_ln_b, conv_w_pw2, conv_b_pw2, rwkv_mu, rwkv_w_rkv, rwkv_w0, rwkv_w1, rwkv_w2, rwkv_a0, rwkv_a1, rwkv_a2, rwkv_g1, rwkv_g2, rwkv_k_k, rwkv_k_a, rwkv_r_k, rwkv_gn_g, rwkv_gn_b, rwkv_w_o, moe_w_group, moe_b_group, moe_w_expert, moe_b_expert, moe_w_gate_up, moe_w_down):
    bp, sp, _ = x_prompt.shape
    bs, ss, _ = x_sample.shape
    assert 1 + bs <= MOD_ROWS
    lay = _Layout(bp, sp, bs, ss)
    x = jnp.concatenate([x_prompt.reshape(bp * sp, D), x_sample.reshape(bs * ss, D)], axis=0)
    cond = jnp.concatenate([c_ctx[None, :], c, jnp.zeros((MOD_ROWS - 1 - bs, D), f32)], axis=0)
    modt = _modulation_table(cond, mod_w, mod_b)
    zero_b = jnp.zeros((D,), f32)
    new_k, new_v, new_sf, new_sb = [], [], [], []
    n_attn = n_conv = n_rwkv = 0
    for i in range(DEPTH):
        m = modt[i]
        kind = i % 3
        if kind == 0:
            j = n_attn
            qkv = _mod_proj(lay, x, m[0], m[1], attn_w_qkv[j], jnp.zeros((Q_DIM + 2 * KV_DIM,), f32), name="attn_qkv")
            past = cache_attn_k.shape[2]
            o_p = _context_attention(lay, qkv, attn_sink[j])
            o_s = _latent_attention(lay, qkv, cache_attn_k[:, j].reshape(bs, past, KV_DIM),
                                    cache_attn_v[:, j].reshape(bs, past, KV_DIM), attn_sink[j])
            y = jnp.concatenate([o_p, o_s], axis=0)
            w_out, b_out = attn_w_o[j], zero_b
            new_k.append(qkv[:lay.n_prompt, Q_DIM:Q_DIM + KV_DIM].reshape(bp, sp, N_KV_HEADS, HEAD_DIM))
            new_v.append(qkv[:lay.n_prompt, Q_DIM + KV_DIM:].reshape(bp, sp, N_KV_HEADS, HEAD_DIM))
            n_attn += 1
        elif kind == 1:
            j = n_conv
            y = _conformer(lay, x, m, conv_w_pw1[j], conv_b_pw1[j], conv_w_dw[j], conv_b_dw[j], conv_ln_g[j],
                           conv_ln_b[j], conv_w_pw2[j], conv_b_pw2[j])
            w_out, b_out = conv_w_pw2[j], conv_b_pw2[j]
            n_conv += 1
        else:
            j = n_rwkv
            y, sf, sb = _rwkv(lay, x, m, state_rwkv_fwd[:, j], state_rwkv_bwd[:, j], rwkv_mu[j], rwkv_w_rkv[j],
                              rwkv_w0[j], rwkv_w1[j], rwkv_w2[j], rwkv_a0[j], rwkv_a1[j], rwkv_a2[j], rwkv_g1[j],
                              rwkv_g2[j], rwkv_k_k[j], rwkv_k_a[j], rwkv_r_k[j], rwkv_gn_g[j], rwkv_gn_b[j])
            w_out, b_out = rwkv_w_o[j], zero_b
            new_sf.append(sf)
            new_sb.append(sb)
            n_rwkv += 1
        x = _out_proj_ln(lay, y, w_out, b_out, x, m[2], post_ln_g[i, 0], post_ln_b[i, 0], name="mixer_out_ln")
        x = _moe_layer(lay, x, m, moe_w_group[i], moe_b_group[i], moe_w_expert[i], moe_b_expert[i],
                       moe_w_gate_up[i], moe_w_down[i], post_ln_g[i, 1], post_ln_b[i, 1])
    y_prompt = x[:lay.n_prompt].reshape(bp, sp, D)
    y_sample = x[lay.n_prompt:].reshape(bs, ss, D)
    return (y_prompt, y_sample, jnp.stack(new_k, axis=1), jnp.stack(new_v, axis=1),
            jnp.stack(new_sf, axis=1).astype(x_prompt.dtype), jnp.stack(new_sb, axis=1).astype(x_prompt.dtype))
```

```python
import functools

import jax
import jax.numpy as jnp
from jax import lax
from jax.experimental import pallas as pl
from jax.experimental.pallas import tpu as pltpu

f32 = jnp.float32
bf16 = jnp.bfloat16

D = 1024
DEPTH = 4
GRID_W = 64
HEAD_DIM = 64
N_Q_HEADS = D // HEAD_DIM
N_KV_HEADS = N_Q_HEADS // 4
Q_PER_KV = N_Q_HEADS // N_KV_HEADS
Q_DIM = N_Q_HEADS * HEAD_DIM
KV_DIM = N_KV_HEADS * HEAD_DIM
WINDOW = 128
ROPE_BASE = 10000.0
CONV_WIDTH = 31
CONV_PAD = (CONV_WIDTH - 1) // 2
RWKV_HEAD = 64
RWKV_HEADS = D // RWKV_HEAD
GN_EPS = 64e-5
N_GROUPS = 4
EXPERTS_PER_GROUP = 8
N_EXPERTS = N_GROUPS * EXPERTS_PER_GROUP
TOP_K = 2
EXPERT_FF = D // 2
LN_EPS = 1e-5
ALPHA = (2 * DEPTH) ** 0.25

LANES = 128
ROW_TILE = 256
MOE_ROWS = 256
ATT_Q = 128
SCAN_CHUNK = 64
SCAN_ROWS = 256
SCAN_PAIRS = 4
MOD_ROWS = 16
VMEM_LIMIT = 56 * 1024 * 1024
NEG = -1e30
HI = lax.Precision.HIGHEST


def _params(n_axes):
    return pltpu.CompilerParams(dimension_semantics=("arbitrary",) * n_axes,
                                vmem_limit_bytes=VMEM_LIMIT)


def _const_spec(shape):
    nd = len(shape)
    return pl.BlockSpec(shape, lambda *_: (0,) * nd)


class _Layout:
    def __init__(self, bp, sp, bs, ss):
        self.bp, self.sp, self.bs, self.ss = bp, sp, bs, ss
        self.n_prompt = bp * sp
        self.n = bp * sp + bs * ss
        assert sp % ROW_TILE == 0 and ss % ROW_TILE == 0
        self.n_tiles = self.n // ROW_TILE
        self.prompt_tiles = self.n_prompt // ROW_TILE
        self.tiles_per_sseq = ss // ROW_TILE
        self.tiles_per_pseq = sp // ROW_TILE

    def mod_row(self, i):
        return jnp.where(i < self.prompt_tiles, 0, 1 + (i - self.prompt_tiles) // self.tiles_per_sseq)

    def seq_pos(self, i):
        in_p = i < self.prompt_tiles
        pos = jnp.where(in_p, i % self.tiles_per_pseq, (i - self.prompt_tiles) % self.tiles_per_sseq)
        cnt = jnp.where(in_p, self.tiles_per_pseq, self.tiles_per_sseq)
        return pos, cnt

    def row_spec(self, width=D, col=0):
        return pl.BlockSpec((ROW_TILE, width), lambda i: (i, col))

    def mod_spec(self):
        return pl.BlockSpec((None, 1, D), lambda i: (self.mod_row(i), 0, 0))


def _layer_norm(z, g, b, eps=LN_EPS):
    mu = jnp.mean(z, axis=-1, keepdims=True)
    zc = z - mu
    var = jnp.mean(zc * zc, axis=-1, keepdims=True)
    return zc * lax.rsqrt(var + eps) * g + b


def _silu(x):
    return x * jax.nn.sigmoid(x)


def _mod_body(c_ref, w_ref, b_ref, o_ref):
    o_ref[...] = jnp.dot(_silu(c_ref[...]), w_ref[...], precision=HI,
                         preferred_element_type=f32) + b_ref[...]


def _modulation_table(cond, mod_w, mod_b):
    out = pl.pallas_call(
        _mod_body,
        grid=(DEPTH, 6),
        in_specs=[_const_spec((MOD_ROWS, D)),
                  pl.BlockSpec((None, D, D), lambda l, j: (l, 0, j)),
                  pl.BlockSpec((None, 1, D), lambda l, j: (l, 0, j))],
        out_specs=pl.BlockSpec((None, None, MOD_ROWS, D), lambda l, j: (l, j, 0, 0)),
        out_shape=jax.ShapeDtypeStruct((DEPTH, 6, MOD_ROWS, D), f32),
        compiler_params=_params(2),
        name="modulation",
    )(cond, mod_w, mod_b.reshape(DEPTH, 1, 6 * D))
    return out.reshape(DEPTH, 6, MOD_ROWS, 1, D)


def _proj_body(x_ref, sh_ref, sc_ref, w_ref, b_ref, o_ref, *, glu):
    h = x_ref[...] * (1.0 + sc_ref[...]) + sh_ref[...]
    u = jnp.dot(h.astype(bf16), w_ref[...], preferred_element_type=f32) + b_ref[...]
    if glu:
        half = u.shape[1] // 2
        u = u[:, :half] * jax.nn.sigmoid(u[:, half:])
    o_ref[...] = u.astype(o_ref.dtype)


def _mod_proj(lay, x, shift, scale, w, b, *, glu=False, out_dtype=f32, name="proj"):
    k, n = w.shape
    n_out = n // 2 if glu else n
    return pl.pallas_call(
        functools.partial(_proj_body, glu=glu),
        grid=(lay.n_tiles,),
        in_specs=[lay.row_spec(), lay.mod_spec(), lay.mod_spec(),
                  _const_spec((k, n)), _const_spec((1, n))],
        out_specs=lay.row_spec(n_out),
        out_shape=jax.ShapeDtypeStruct((lay.n, n_out), out_dtype),
        compiler_params=_params(1),
        name=name,
    )(x, shift, scale, w.astype(bf16), b.reshape(1, n))


def _out_ln_body(y_ref, w_ref, b_ref, x_ref, gate_ref, g_ref, be_ref, o_ref):
    t = jnp.dot(y_ref[...].astype(bf16), w_ref[...], preferred_element_type=f32) + b_ref[...]
    z = ALPHA * x_ref[...] + gate_ref[...] * t
    o_ref[...] = _layer_norm(z, g_ref[...], be_ref[...])


def _out_proj_ln(lay, y, w, b, x, gate, ln_g, ln_b, name="out_ln"):
    k = w.shape[0]
    return pl.pallas_call(
        _out_ln_body,
        grid=(lay.n_tiles,),
        in_specs=[lay.row_spec(k), _const_spec((k, D)), _const_spec((1, D)),
                  lay.row_spec(), lay.mod_spec(), _const_spec((1, D)), _const_spec((1, D))],
        out_specs=lay.row_spec(),
        out_shape=jax.ShapeDtypeStruct((lay.n, D), f32),
        compiler_params=_params(1),
        name=name,
    )(y, w.astype(bf16), b.reshape(1, D), x, gate, ln_g.reshape(1, D), ln_b.reshape(1, D))


def _sink_column(sink_ref, h, rows_per_head):
    r = lax.broadcasted_iota(jnp.int32, (Q_PER_KV * rows_per_head, 1), 0) // rows_per_head
    col = jnp.full(r.shape, sink_ref[h * Q_PER_KV], f32)
    for g in range(1, Q_PER_KV):
        col = jnp.where(r == g, sink_ref[h * Q_PER_KV + g], col)
    return col


def _stack_heads(q, h):
    return jnp.concatenate(
        [q[:, (h * Q_PER_KV + g) * HEAD_DIM:(h * Q_PER_KV + g + 1) * HEAD_DIM] for g in range(Q_PER_KV)], axis=0)


def _nt(a, b):
    return lax.dot_general(a, b, (((1,), (1,)), ((), ())), preferred_element_type=f32)


def _ctx_attn_body(sink_ref, q_ref, k_ref, v_ref, o_ref):
    rows = q_ref.shape[0]
    q = q_ref[...] * (HEAD_DIM ** -0.5)
    k = k_ref[...]
    v = v_ref[...]
    outs = [None] * N_Q_HEADS
    for h in range(N_KV_HEADS):
        qh = _stack_heads(q, h).astype(bf16)
        kh = k[:, h * HEAD_DIM:(h + 1) * HEAD_DIM].astype(bf16)
        vh = v[:, h * HEAD_DIM:(h + 1) * HEAD_DIM].astype(bf16)
        s = _nt(qh, kh)
        sk = _sink_column(sink_ref, h, rows)
        m = jnp.maximum(jnp.max(s, axis=-1, keepdims=True), sk)
        p = jnp.exp(s - m)
        den = jnp.sum(p, axis=-1, keepdims=True) + jnp.exp(sk - m)
        o = jnp.dot(p.astype(bf16), vh, preferred_element_type=f32) / den
        for g in range(Q_PER_KV):
            outs[h * Q_PER_KV + g] = o[g * rows:(g + 1) * rows]
    o_ref[...] = jnp.concatenate(outs, axis=1).astype(o_ref.dtype)


def _context_attention(lay, qkv, sink):
    sp = lay.sp
    kcol = Q_DIM // KV_DIM
    return pl.pallas_call(
        _ctx_attn_body,
        grid=(lay.bp,),
        in_specs=[pl.BlockSpec(memory_space=pltpu.SMEM),
                  pl.BlockSpec((sp, Q_DIM), lambda b: (b, 0)),
                  pl.BlockSpec((sp, KV_DIM), lambda b: (b, kcol)),
                  pl.BlockSpec((sp, KV_DIM), lambda b: (b, kcol + 1))],
        out_specs=pl.BlockSpec((sp, Q_DIM), lambda b: (b, 0)),
        out_shape=jax.ShapeDtypeStruct((lay.n_prompt, Q_DIM), bf16),
        compiler_params=_params(1),
        name="ctx_attn",
    )(sink, qkv, qkv, qkv)


def _rope_tables(t_len):
    t = jnp.arange(t_len)
    row = (t // GRID_W).astype(f32)
    col = (t % GRID_W).astype(f32)
    quarter = HEAD_DIM // 4
    d = jnp.arange(LANES) % HEAD_DIM
    inv = ROPE_BASE ** (-(d % quarter).astype(f32) / quarter)
    pos = jnp.where((d < HEAD_DIM // 2)[None, :], row[:, None], col[:, None])
    ang = pos * inv[None, :]
    cs, sn = jnp.cos(ang), jnp.sin(ang)
    first = ((d % (2 * quarter)) < quarter)[None, :]
    return cs, jnp.where(first, -sn, 0.0), jnp.where(first, 0.0, sn)


def _rope(x, cs, sa, sb):
    quarter = HEAD_DIM // 4
    outs = []
    for c in range(x.shape[1] // LANES):
        xc = x[:, c * LANES:(c + 1) * LANES]
        outs.append(xc * cs + pltpu.roll(xc, LANES - quarter, axis=1) * sa + pltpu.roll(xc, quarter, axis=1) * sb)
    return outs[0] if len(outs) == 1 else jnp.concatenate(outs, axis=1)


def _lat_attn_body(sink_ref, q_ref, k_ref, v_ref, kc_ref, vc_ref, cs_ref, sa_ref, sb_ref, o_ref):
    j = pl.program_id(1)
    t_len = k_ref.shape[0]
    win = 3 * ATT_Q
    start = pl.multiple_of(jnp.clip((j - 1) * ATT_Q, 0, t_len - win), ATT_Q)
    q0 = pl.multiple_of(j * ATT_Q, ATT_Q)
    q = _rope(q_ref[...], cs_ref[pl.ds(q0, ATT_Q), :], sa_ref[pl.ds(q0, ATT_Q), :],
              sb_ref[pl.ds(q0, ATT_Q), :]) * (HEAD_DIM ** -0.5)
    kw = _rope(k_ref[pl.ds(start, win), :], cs_ref[pl.ds(start, win), :], sa_ref[pl.ds(start, win), :],
               sb_ref[pl.ds(start, win), :])
    vw = v_ref[pl.ds(start, win), :]
    kc = kc_ref[...]
    vc = vc_ref[...]
    rows = Q_PER_KV * ATT_Q
    qpos = q0 + lax.broadcasted_iota(jnp.int32, (rows, win), 0) % ATT_Q
    kpos = start + lax.broadcasted_iota(jnp.int32, (rows, win), 1)
    valid = jnp.abs(qpos - kpos) <= WINDOW
    outs = [None] * N_Q_HEADS
    for h in range(N_KV_HEADS):
        sl = slice(h * HEAD_DIM, (h + 1) * HEAD_DIM)
        qh = _stack_heads(q, h).astype(bf16)
        s_c = _nt(qh, kc[:, sl].astype(bf16))
        s_l = jnp.where(valid, _nt(qh, kw[:, sl].astype(bf16)), NEG)
        sk = _sink_column(sink_ref, h, ATT_Q)
        m = jnp.maximum(jnp.maximum(jnp.max(s_c, axis=-1, keepdims=True),
                                    jnp.max(s_l, axis=-1, keepdims=True)), sk)
        p_c = jnp.exp(s_c - m)
        p_l = jnp.exp(s_l - m)
        den = jnp.sum(p_c, axis=-1, keepdims=True) + jnp.sum(p_l, axis=-1, keepdims=True) + jnp.exp(sk - m)
        o = (jnp.dot(p_c.astype(bf16), vc[:, sl].astype(bf16), preferred_element_type=f32)
             + jnp.dot(p_l.astype(bf16), vw[:, sl].astype(bf16), preferred_element_type=f32)) / den
        for g in range(Q_PER_KV):
            outs[h * Q_PER_KV + g] = o[g * ATT_Q:(g + 1) * ATT_Q]
    o_ref[...] = jnp.concatenate(outs, axis=1).astype(o_ref.dtype)


def _latent_attention(lay, qkv, kc, vc, sink):
    ss = lay.ss
    past = kc.shape[1]
    nq = ss // ATT_Q
    q_base = lay.n_prompt // ATT_Q
    s_base = lay.n_prompt // ss
    assert lay.n_prompt % ss == 0
    kcol = Q_DIM // KV_DIM
    cs, sa, sb = _rope_tables(ss)
    return pl.pallas_call(
        _lat_attn_body,
        grid=(lay.bs, nq),
        in_specs=[pl.BlockSpec(memory_space=pltpu.SMEM),
                  pl.BlockSpec((ATT_Q, Q_DIM), lambda b, j: (q_base + b * nq + j, 0)),
                  pl.BlockSpec((ss, KV_DIM), lambda b, j: (s_base + b, kcol)),
                  pl.BlockSpec((ss, KV_DIM), lambda b, j: (s_base + b, kcol + 1)),
                  pl.BlockSpec((None, past, KV_DIM), lambda b, j: (b, 0, 0)),
                  pl.BlockSpec((None, past, KV_DIM), lambda b, j: (b, 0, 0)),
                  _const_spec((ss, LANES)), _const_spec((ss, LANES)), _const_spec((ss, LANES))],
        out_specs=pl.BlockSpec((ATT_Q, Q_DIM), lambda b, j: (b * nq + j, 0)),
        out_shape=jax.ShapeDtypeStruct((lay.bs * ss, Q_DIM), bf16),
        compiler_params=_params(2),
        name="lat_attn",
    )(sink, qkv, qkv, qkv, kc, vc, cs, sa, sb)


CONV_HALO = 16
CONV_SUB = 32


def _dwconv_body(lay, u_ref, prev_ref, next_ref, w_ref, b_ref, g_ref, be_ref, o_ref, buf_ref):
    i = pl.program_id(0)
    pos, cnt = lay.seq_pos(i)
    buf_ref[pl.ds(0, CONV_HALO), :] = jnp.where(pos > 0, prev_ref[...], 0.0)
    buf_ref[pl.ds(CONV_HALO, ROW_TILE), :] = u_ref[...]
    buf_ref[pl.ds(CONV_HALO + ROW_TILE, CONV_HALO), :] = jnp.where(pos < cnt - 1, next_ref[...], 0.0)
    w = w_ref[...]
    off = CONV_HALO - CONV_PAD

    for s in range(ROW_TILE // CONV_SUB):
        r0 = s * CONV_SUB
        acc = jnp.zeros((CONV_SUB, D), f32)
        for t in range(CONV_WIDTH):
            acc = acc + buf_ref[pl.ds(r0 + off + t, CONV_SUB), :] * w[t:t + 1, :]
        z = _layer_norm(acc + b_ref[...], g_ref[...], be_ref[...])
        o_ref[pl.ds(r0, CONV_SUB), :] = _silu(z).astype(o_ref.dtype)


def _dwconv_ln_swish(lay, u, w_dw, b_dw, ln_g, ln_b):
    hpt = ROW_TILE // CONV_HALO
    last = lay.n // CONV_HALO - 1
    return pl.pallas_call(
        functools.partial(_dwconv_body, lay),
        grid=(lay.n_tiles,),
        in_specs=[lay.row_spec(),
                  pl.BlockSpec((CONV_HALO, D), lambda i: (jnp.maximum(i * hpt - 1, 0), 0)),
                  pl.BlockSpec((CONV_HALO, D), lambda i: (jnp.minimum((i + 1) * hpt, last), 0)),
                  _const_spec((CONV_WIDTH, D)), _const_spec((1, D)), _const_spec((1, D)), _const_spec((1, D))],
        out_specs=lay.row_spec(),
        out_shape=jax.ShapeDtypeStruct((lay.n, D), bf16),
        scratch_shapes=[pltpu.VMEM((ROW_TILE + 2 * CONV_HALO, D), f32)],
        compiler_params=_params(1),
        name="dwconv",
    )(u, u, u, w_dw, b_dw.reshape(1, D), ln_g.reshape(1, D), ln_b.reshape(1, D))


def _conformer(lay, x, m, w_pw1, b_pw1, w_dw, b_dw, ln_g, ln_b, w_pw2, b_pw2):
    u = _mod_proj(lay, x, m[0], m[1], w_pw1, b_pw1, glu=True, name="conv_pw1_glu")
    return _dwconv_ln_swish(lay, u, w_dw, b_dw, ln_g, ln_b)


LORA_PAD = 128
GATE_PAD = 256


def _bdot(a, w):
    return jnp.dot(a.astype(bf16), w, preferred_element_type=f32)


def _rwkv_proj_body(lay, x_ref, prev_ref, next_ref, sh_ref, sc_ref, mu_ref, wrkv_ref, w0_ref, w1_ref, w2_ref,
                    a0_ref, a1_ref, a2_ref, g1_ref, g2_ref, r_ref, k_ref, v_ref, g_ref, lw_ref, a_ref):
    i = pl.program_id(0)
    pos, cnt = lay.seq_pos(i)
    sc = 1.0 + sc_ref[...]
    sh = sh_ref[...]
    h = x_ref[...] * sc + sh
    hb = prev_ref.shape[0]
    h_prev = jnp.where(pos > 0, prev_ref[hb - 1:hb, :] * sc + sh, 0.0)
    h_next = jnp.where(pos < cnt - 1, next_ref[0:1, :] * sc + sh, 0.0)
    row = lax.broadcasted_iota(jnp.int32, (ROW_TILE, 1), 0)
    below = jnp.where(row == 0, h_prev, pltpu.roll(h, 1, axis=0))
    above = jnp.where(row == ROW_TILE - 1, h_next, pltpu.roll(h, ROW_TILE - 1, axis=0))
    xx = 0.5 * (below + above) - h

    def mix(j):
        return (h + xx * mu_ref[j:j + 1, :]).astype(bf16)

    r_ref[...] = _bdot(mix(0), wrkv_ref[0])
    k_ref[...] = _bdot(mix(2), wrkv_ref[1])
    v_ref[...] = _bdot(mix(3), wrkv_ref[2])
    g_ref[...] = _bdot(jax.nn.sigmoid(_bdot(mix(5), g1_ref[...])), g2_ref[...])
    xw = mix(1)
    xa = mix(4)
    for d in range(2):
        z = w0_ref[d:d + 1, :] + _bdot(jnp.tanh(_bdot(xw, w1_ref[d])), w2_ref[d])
        softplus = jnp.maximum(-z, 0.0) + jnp.log(1.0 + jnp.exp(-jnp.abs(z)))
        lw_ref[d] = -jnp.exp(-softplus - 0.5)
        a_ref[d] = jax.nn.sigmoid(a0_ref[d:d + 1, :] + _bdot(_bdot(xa, a1_ref[d]), a2_ref[d]))


def _pad_to(w, axis, size):
    pad = [(0, 0)] * w.ndim
    pad[axis] = (0, size - w.shape[axis])
    return jnp.pad(w, pad)


def _rwkv_project(lay, x, m, mu, w_rkv, w0, w1, w2, a0, a1, a2, g1, g2):
    halo = 8
    hpt = ROW_TILE // halo
    last = lay.n // halo - 1
    nd = (lay.n, D)
    consts = [mu, w_rkv.astype(bf16), w0,
              _pad_to(w1, 2, LORA_PAD).astype(bf16), _pad_to(w2, 1, LORA_PAD).astype(bf16), a0,
              _pad_to(a1, 2, LORA_PAD).astype(bf16), _pad_to(a2, 1, LORA_PAD).astype(bf16),
              _pad_to(g1, 1, GATE_PAD).astype(bf16), _pad_to(g2, 0, GATE_PAD).astype(bf16)]
    return pl.pallas_call(
        functools.partial(_rwkv_proj_body, lay),
        grid=(lay.n_tiles,),
        in_specs=[lay.row_spec(),
                  pl.BlockSpec((halo, D), lambda i: (jnp.maximum(i * hpt - 1, 0), 0)),
                  pl.BlockSpec((halo, D), lambda i: (jnp.minimum((i + 1) * hpt, last), 0)),
                  lay.mod_spec(), lay.mod_spec()] + [_const_spec(c.shape) for c in consts],
        out_specs=[lay.row_spec()] * 4 + [pl.BlockSpec((2, ROW_TILE, D), lambda i: (0, i, 0))] * 2,
        out_shape=[jax.ShapeDtypeStruct(nd, f32)] * 4 + [jax.ShapeDtypeStruct((2,) + nd, f32)] * 2,
        compiler_params=_params(1),
        name="rwkv_proj",
    )(x, x, x, m[0], m[1], *consts)


def _split_dot(x, c, parts):
    acc = None
    for _ in range(parts):
        hi = x.astype(bf16)
        t = jnp.dot(hi, c, preferred_element_type=f32)
        acc = t if acc is None else acc + t
        x = x - hi.astype(f32)
    return acc


def _split_dot_left(c, x, parts):
    acc = None
    for _ in range(parts):
        hi = x.astype(bf16)
        t = jnp.dot(c, hi, preferred_element_type=f32)
        acc = t if acc is None else acc + t
        x = x - hi.astype(f32)
    return acc


def _mm(a, b):
    return jnp.dot(a.astype(bf16), b.astype(bf16), preferred_element_type=f32)


def _mmh(a, b):
    return jnp.dot(a, b, precision=HI, preferred_element_type=f32)


def _mm_nt(a, b):
    return lax.dot_general(a.astype(bf16), b.astype(bf16), (((1,), (1,)), ((), ())), preferred_element_type=f32)


def _scan_body(lay, rev, r_ref, k_ref, v_ref, lw_ref, a_ref, kk_ref, ka_ref, rk_ref, s0_ref, tri_ref, bd_ref,
               y_ref, bon_ref, sfin_ref, state_ref, t_s, rhs_s, vst_s, ar_s, arbk_s, bkh_s, etot_s):
    t_idx = pl.program_id(1)
    i = (lay.n_tiles - 1 - t_idx) if rev else t_idx
    pos, cnt = lay.seq_pos(i)
    first = (pos == cnt - 1) if rev else (pos == 0)
    final = (pos == 0) if rev else (pos == cnt - 1)
    c_len = SCAN_CHUNK
    n_chunks = SCAN_ROWS // c_len

    @pl.when(first)
    def _():
        state_ref[...] = s0_ref[...]

    lane = lax.broadcasted_iota(jnp.int32, (1, LANES), 1)
    head0 = lane < RWKV_HEAD
    rr = lax.broadcasted_iota(jnp.int32, (2 * c_len, 2 * c_len), 0)
    cc = lax.broadcasted_iota(jnp.int32, (2 * c_len, 2 * c_len), 1)
    same = (rr // c_len) == (cc // c_len)
    tt = rr % c_len
    ss = cc % c_len
    strict = same & ((tt < ss) if rev else (tt > ss))
    incl = same & ((tt <= ss) if rev else (tt >= ss))
    tri = tri_ref[...]
    bd = bd_ref[...]

    def stack(xc):
        return jnp.concatenate([jnp.where(head0, xc, 0.0), jnp.where(head0, 0.0, xc)], axis=0)

    def pair_chunk(rows, slot, p):
        cols = slice(p * LANES, (p + 1) * LANES)
        r = r_ref[rows, cols]
        k = k_ref[rows, cols]
        v = v_ref[rows, cols]
        lw = lw_ref[rows, cols]
        a = a_ref[rows, cols]
        kk = k * kk_ref[:, cols]
        nrm = jnp.sqrt(_split_dot(kk * kk, bd, 3))
        kk = kk / jnp.maximum(nrm, 1e-12)
        b = kk * a
        kd = k * (1.0 + (a - 1.0) * ka_ref[:, cols])
        bon_ref[rows, cols] = _split_dot(r * kd * rk_ref[:, cols], bd, 3) * v
        cum = _split_dot_left(tri, lw, 3)
        tot = cum[0:1, :] if rev else cum[c_len - 1:c_len, :]
        e_tot = jnp.exp(tot)
        e_neg = jnp.exp(-cum)
        a_t = -kk * jnp.exp(cum - lw)
        r_t = r * jnp.exp(cum)
        b_t = b * e_neg
        k_t = kd * e_neg
        ar = jnp.concatenate([stack(a_t), stack(r_t)], axis=0)
        bk = jnp.concatenate([stack(b_t), stack(k_t)], axis=0)
        m = _mm_nt(ar, bk)
        h2 = 2 * c_len
        n_ab = jnp.where(strict, m[:h2, :h2], 0.0)
        a_ak = jnp.where(strict, m[:h2, h2:], 0.0)
        a_rb = jnp.where(incl, m[h2:, :h2], 0.0)
        a_rk = jnp.where(incl, m[h2:, h2:], 0.0)
        v_st = stack(v)
        t = jnp.where(rr == cc, 1.0, 0.0) + n_ab
        pw = _mmh(n_ab, n_ab)
        span = 2
        while 2 * span < c_len:
            both = _mmh(pw, jnp.concatenate([t, pw], axis=1))
            t = t + both[:, :h2]
            pw = both[:, h2:]
            span *= 2
        t = t + _mmh(pw, t)
        t_s[slot, p] = t
        rhs_s[slot, p] = _mm(a_ak, v_st)
        vst_s[slot, p] = v_st
        ar_s[slot, p] = ar.astype(bf16)
        arbk_s[slot, p] = jnp.concatenate([a_rb, a_rk], axis=1).astype(bf16)
        bkh_s[slot, p] = (bk * e_tot).astype(bf16)
        etot_s[slot, p] = e_tot

    def pair_apply(slot, rows, p):
        cols = slice(p * LANES, (p + 1) * LANES)
        h2 = 2 * c_len
        state = state_ref[p]
        x0 = lax.dot_general(ar_s[slot, p], state.astype(bf16), (((1,), (1,)), ((), ())),
                             preferred_element_type=f32)
        sa = _mmh(t_s[slot, p], x0[:h2] + rhs_s[slot, p])
        sav = jnp.concatenate([sa, vst_s[slot, p]], axis=0)
        y_st = x0[h2:] + jnp.dot(arbk_s[slot, p], sav.astype(bf16), preferred_element_type=f32)
        y_ref[rows, cols] = y_st[:c_len] + y_st[c_len:]
        state_ref[p] = state * etot_s[slot, p] + jnp.dot(sav.T.astype(bf16), bkh_s[slot, p],
                                                        preferred_element_type=f32)

    def prepare(c2, carry):
        for u in range(2):
            slot = c2 * 2 + u
            rows = pl.ds(pl.multiple_of(slot * c_len, c_len), c_len)
            for p in range(SCAN_PAIRS):
                pair_chunk(rows, slot, p)
        return carry

    def chunk(c, carry):
        slot = (n_chunks - 1 - c) if rev else c
        rows = pl.ds(pl.multiple_of(slot * c_len, c_len), c_len)
        for p in range(SCAN_PAIRS):
            pair_apply(slot, rows, p)
        return carry

    lax.fori_loop(0, n_chunks // 2, prepare, 0)
    lax.fori_loop(0, n_chunks, chunk, 0)

    @pl.when(final)
    def _():
        sfin_ref[...] = state_ref[...]


def _block_diag_states(s):
    b = s.shape[0]
    s = s.reshape(b, RWKV_HEADS // 2, 2, RWKV_HEAD, RWKV_HEAD)
    eye = jnp.eye(2, dtype=s.dtype)
    out = s[:, :, :, :, None, :] * eye[None, None, :, None, :, None]
    return out.reshape(b, RWKV_HEADS // 2, LANES, LANES)


def _diag_blocks(w):
    b = w.shape[0]
    w = w.reshape(b, RWKV_HEADS // 2, 2, RWKV_HEAD, 2, RWKV_HEAD)
    return jnp.stack([w[:, :, 0, :, 0, :], w[:, :, 1, :, 1, :]], axis=2).reshape(b, RWKV_HEADS, RWKV_HEAD, RWKV_HEAD)


def _rwkv_scan(lay, rev, r, k, v, lw, a, k_k, k_a, r_k, s0):
    d = 1 if rev else 0
    n_seq = lay.bp + lay.bs
    c_len = SCAN_CHUNK
    idx = jnp.arange(c_len)
    tri = ((idx[:, None] <= idx[None, :]) if rev else (idx[:, None] >= idx[None, :])).astype(bf16)
    hh = jnp.arange(LANES) // RWKV_HEAD
    bd = (hh[:, None] == hh[None, :]).astype(bf16)

    def tile(t):
        return (lay.n_tiles - 1 - t) if rev else t

    def seq(t):
        i = tile(t)
        return jnp.where(i < lay.prompt_tiles, i // lay.tiles_per_pseq,
                         lay.bp + (i - lay.prompt_tiles) // lay.tiles_per_sseq)

    width = SCAN_PAIRS * LANES
    per = (SCAN_ROWS // c_len, SCAN_PAIRS)
    row = pl.BlockSpec((SCAN_ROWS, width), lambda p, t: (tile(t), p))
    row_d = pl.BlockSpec((None, SCAN_ROWS, width), lambda p, t: (d, tile(t), p))
    par = pl.BlockSpec((1, width), lambda p, t: (0, p))
    st = pl.BlockSpec((None, SCAN_PAIRS, LANES, LANES), lambda p, t: (seq(t), p, 0, 0))
    return pl.pallas_call(
        functools.partial(_scan_body, lay, rev),
        grid=(RWKV_HEADS // 2 // SCAN_PAIRS, lay.n_tiles),
        in_specs=[row, row, row, row_d, row_d, par, par, par, st,
                  _const_spec((c_len, c_len)), _const_spec((LANES, LANES))],
        out_specs=[row, row, st],
        out_shape=[jax.ShapeDtypeStruct((lay.n, D), f32), jax.ShapeDtypeStruct((lay.n, D), f32),
                   jax.ShapeDtypeStruct((n_seq, RWKV_HEADS // 2, LANES, LANES), f32)],
        scratch_shapes=[pltpu.VMEM((SCAN_PAIRS, LANES, LANES), f32),
                        pltpu.VMEM(per + (2 * c_len, 2 * c_len), f32),
                        pltpu.VMEM(per + (2 * c_len, LANES), f32),
                        pltpu.VMEM(per + (2 * c_len, LANES), f32),
                        pltpu.VMEM(per + (4 * c_len, LANES), bf16),
                        pltpu.VMEM(per + (2 * c_len, 4 * c_len), bf16),
                        pltpu.VMEM(per + (4 * c_len, LANES), bf16),
                        pltpu.VMEM(per + (1, LANES), f32)],
        compiler_params=_params(2),
        name="rwkv_scan_bwd" if rev else "rwkv_scan_fwd",
    )(r, k, v, lw, a, k_k.reshape(1, D), k_a.reshape(1, D), r_k.reshape(1, D), s0, tri, bd)


def _rwkv_out_body(yf_ref, yb_ref, bf_ref, bb_ref, g_ref, gg_ref, gb_ref, bd_ref, o_ref):
    y = yf_ref[...] + yb_ref[...]
    bd = bd_ref[...]
    inv = 1.0 / RWKV_HEAD
    outs = []
    for c in range(D // LANES):
        yc = y[:, c * LANES:(c + 1) * LANES]
        cen = yc - _split_dot(yc, bd, 3) * inv
        var = _split_dot(cen * cen, bd, 3) * inv
        outs.append(cen * lax.rsqrt(var + GN_EPS))
    yn = jnp.concatenate(outs, axis=1) * gg_ref[...] + gb_ref[...]
    o_ref[...] = ((yn + bf_ref[...] + bb_ref[...]) * g_ref[...]).astype(o_ref.dtype)


def _rwkv(lay, x, m, s0_f, s0_b, mu, w_rkv, w0, w1, w2, a0, a1, a2, g1, g2, k_k, k_a, r_k, gn_g, gn_b):
    r, k, v, g, lw, a = _rwkv_project(lay, x, m, mu, w_rkv, w0, w1, w2, a0, a1, a2, g1, g2)
    zeros = jnp.zeros((lay.bp, RWKV_HEADS // 2, LANES, LANES), f32)
    outs = []
    for rev, s0 in ((False, s0_f), (True, s0_b)):
        s_all = jnp.concatenate([zeros, _block_diag_states(s0.astype(f32))], axis=0)
        outs.append(_rwkv_scan(lay, rev, r, k, v, lw, a, k_k, k_a, r_k, s_all))
    hh = jnp.arange(LANES) // RWKV_HEAD
    bd = (hh[:, None] == hh[None, :]).astype(bf16)
    y = pl.pallas_call(
        _rwkv_out_body,
        grid=(lay.n_tiles,),
        in_specs=[lay.row_spec()] * 5 + [_const_spec((1, D)), _const_spec((1, D)), _const_spec((LANES, LANES))],
        out_specs=lay.row_spec(),
        out_shape=jax.ShapeDtypeStruct((lay.n, D), bf16),
        compiler_params=_params(1),
        name="rwkv_groupnorm_gate",
    )(outs[0][0], outs[1][0], outs[0][1], outs[1][1], g, gn_g.reshape(1, D), gn_b.reshape(1, D), bd)
    return y, _diag_blocks(outs[0][2][:lay.bp]), _diag_blocks(outs[1][2][:lay.bp])


def _router_body(x_ref, sh_ref, sc_ref, w_ref, b_ref, h_ref, route_ref):
    h = x_ref[...] * (1.0 + sc_ref[...]) + sh_ref[...]
    h_ref[...] = h.astype(h_ref.dtype)
    lg = jnp.dot(h, w_ref[...], precision=HI, preferred_element_type=f32) + b_ref[...]
    lane = lax.broadcasted_iota(jnp.int32, lg.shape, 1)

    def top1(vals):
        mx = jnp.max(vals, axis=-1, keepdims=True)
        idx = jnp.min(jnp.where(vals == mx, lane, LANES), axis=-1, keepdims=True)
        return mx, idx

    gl = jnp.where(lane < N_GROUPS, lg, NEG)
    g_max, g_idx = top1(gl)
    p_g = 1.0 / jnp.sum(jnp.exp(gl - g_max), axis=-1, keepdims=True)
    lo = N_GROUPS + g_idx * EXPERTS_PER_GROUP
    el = jnp.where((lane >= lo) & (lane < lo + EXPERTS_PER_GROUP), lg, NEG)
    m1, i1 = top1(el)
    m2, i2 = top1(jnp.where(lane == i1, NEG, el))
    e2 = jnp.exp(m2 - m1)
    w1 = p_g / (1.0 + e2)
    w2 = p_g * e2 / (1.0 + e2)
    route = jnp.where(lane == 0, (i1 - N_GROUPS).astype(f32),
                      jnp.where(lane == 1, (i2 - N_GROUPS).astype(f32),
                                jnp.where(lane == 2, w1, jnp.where(lane == 3, w2, 0.0))))
    route_ref[...] = route


def _expert_body(be_ref, nu_ref, x_ref, wgu_ref, wd_ref, o_ref, wgu_s, wd_s):
    i = pl.program_id(0)
    used = i < nu_ref[0]

    @pl.when(used & ((i == 0) | (be_ref[i] != be_ref[jnp.maximum(i - 1, 0)])))
    def _():
        wgu_s[...] = wgu_ref[...].astype(bf16)
        wd_s[...] = wd_ref[...].astype(bf16)

    @pl.when(used)
    def _():
        u = jnp.dot(x_ref[...], wgu_s[...], preferred_element_type=f32)
        act = _silu(u[:, :EXPERT_FF]) * u[:, EXPERT_FF:]
        o_ref[...] = jnp.dot(act.astype(bf16), wd_s[...], preferred_element_type=f32)

    @pl.when(jnp.logical_not(used))
    def _():
        o_ref[...] = jnp.zeros_like(o_ref)


def _moe_out_body(y_ref0, y_ref1, route_ref, x_ref, gate_ref, g_ref, be_ref, o_ref):
    route = route_ref[...]
    y = route[:, 2:3] * y_ref0[...] + route[:, 3:4] * y_ref1[...]
    z = ALPHA * x_ref[...] + gate_ref[...] * y
    o_ref[...] = _layer_norm(z, g_ref[...], be_ref[...])


def _moe_layer(lay, x, m, w_group, b_group, w_expert, b_expert, w_gu, w_down, ln_g, ln_b):
    n = lay.n
    n_route = N_GROUPS + N_EXPERTS
    w_r = _pad_to(jnp.concatenate([w_group, w_expert], axis=1).astype(f32), 1, LANES)
    b_r = _pad_to(jnp.concatenate([b_group, b_expert]).astype(f32).reshape(1, n_route), 1, LANES)
    h, route = pl.pallas_call(
        _router_body,
        grid=(lay.n_tiles,),
        in_specs=[lay.row_spec(), lay.mod_spec(), lay.mod_spec(), _const_spec((D, LANES)), _const_spec((1, LANES))],
        out_specs=[lay.row_spec(), lay.row_spec(LANES)],
        out_shape=[jax.ShapeDtypeStruct((n, D), bf16), jax.ShapeDtypeStruct((n, LANES), f32)],
        compiler_params=_params(1),
        name="moe_router",
    )(x, m[3], m[4], w_r, b_r)

    s = n * TOP_K
    e_flat = route[:, :TOP_K].astype(jnp.int32).reshape(-1)
    pair = jnp.arange(s, dtype=jnp.int32)
    e_sorted, order = lax.sort((e_flat, pair), num_keys=1, is_stable=True)
    experts = jnp.arange(N_EXPERTS, dtype=jnp.int32)
    counts = jnp.sum((e_flat[:, None] == experts[None, :]).astype(jnp.int32), axis=0)
    padded = (counts + MOE_ROWS - 1) // MOE_ROWS * MOE_ROWS
    pad_end = jnp.cumsum(padded)
    pad_start = pad_end - padded
    start = jnp.cumsum(counts) - counts
    dest_sorted = pad_start[e_sorted] + pair - start[e_sorted]
    n_blocks = -(-s // MOE_ROWS) + N_EXPERTS
    p_rows = n_blocks * MOE_ROWS
    block_start = jnp.arange(n_blocks, dtype=jnp.int32) * MOE_ROWS
    block_expert = jnp.minimum(jnp.sum((pad_end[None, :] <= block_start[:, None]).astype(jnp.int32), axis=1),
                               N_EXPERTS - 1)
    n_used = (pad_end[N_EXPERTS - 1] // MOE_ROWS).astype(jnp.int32).reshape(1)
    slot = jnp.arange(p_rows, dtype=jnp.int32)
    slot_expert = jnp.repeat(block_expert, MOE_ROWS)
    in_run = jnp.minimum(slot - pad_start[slot_expert], jnp.maximum(counts[slot_expert] - 1, 0))
    slot_token = order[jnp.clip(start[slot_expert] + in_run, 0, s - 1)] // TOP_K
    _, dest = lax.sort((order, dest_sorted), num_keys=1)
    dest = dest.reshape(n, TOP_K)

    xb = jnp.take(h, slot_token, axis=0)
    yb = pl.pallas_call(
        _expert_body,
        grid_spec=pltpu.PrefetchScalarGridSpec(
            num_scalar_prefetch=2,
            grid=(n_blocks,),
            in_specs=[pl.BlockSpec((MOE_ROWS, D), lambda i, be, nu: (jnp.minimum(i, nu[0] - 1), 0)),
                      pl.BlockSpec((None, D, 2 * EXPERT_FF), lambda i, be, nu: (be[i], 0, 0)),
                      pl.BlockSpec((None, EXPERT_FF, D), lambda i, be, nu: (be[i], 0, 0))],
            out_specs=pl.BlockSpec((MOE_ROWS, D), lambda i, be, nu: (i, 0)),
            scratch_shapes=[pltpu.VMEM((D, 2 * EXPERT_FF), bf16), pltpu.VMEM((EXPERT_FF, D), bf16)]),
        out_shape=jax.ShapeDtypeStruct((p_rows, D), f32),
        compiler_params=_params(1),
        name="moe_experts",
    )(block_expert, n_used, xb, w_gu, w_down)
    y0 = jnp.take(yb, dest[:, 0], axis=0)
    y1 = jnp.take(yb, dest[:, 1], axis=0)

    return pl.pallas_call(
        _moe_out_body,
        grid=(lay.n_tiles,),
        in_specs=[lay.row_spec(), lay.row_spec(), lay.row_spec(LANES), lay.row_spec(), lay.mod_spec(),
                  _const_spec((1, D)), _const_spec((1, D))],
        out_specs=lay.row_spec(),
        out_shape=jax.ShapeDtypeStruct((n, D), f32),
        compiler_params=_params(1),
        name="moe_combine_ln",
    )(y0, y1, route, x, m[5], ln_g.reshape(1, D), ln_b.reshape(1, D))


def kernel(x_prompt, x_sample, c, c_ctx, cache_attn_k, cache_attn_v, state_rwkv_fwd, state_rwkv_bwd, mod_w, mod_b, post_ln_g, post_ln_b, attn_w_qkv, attn_w_o, attn_sink, conv_w_pw1, conv_b_pw1, conv_w_dw, conv_b_dw, conv_ln_g, conv_ln_b, conv_w_pw2, conv_b_pw2, rwkv_mu, rwkv_w_rkv, rwkv_w0, rwkv_w1, rwkv_w2, rwkv_a0, rwkv_a1, rwkv_a2, rwkv_g1, rwkv_g2, rwkv_k_k, rwkv_k_a, rwkv_r_k, rwkv_gn_g, rwkv_gn_b, rwkv_w_o, moe_w_group, moe_b_group, moe_w_expert, moe_b_expert, moe_w_gate_up, moe_w_down):
    bp, sp, _ = x_prompt.shape
    bs, ss, _ = x_sample.shape
    assert 1 + bs <= MOD_ROWS
    lay = _Layout(bp, sp, bs, ss)
    x = jnp.concatenate([x_prompt.reshape(bp * sp, D), x_sample.reshape(bs * ss, D)], axis=0)
    cond = jnp.concatenate([c_ctx[None, :], c, jnp.zeros((MOD_ROWS - 1 - bs, D), f32)], axis=0)
    modt = _modulation_table(cond, mod_w, mod_b)
    zero_b = jnp.zeros((D,), f32)
    new_k, new_v, new_sf, new_sb = [], [], [], []
    n_attn = n_conv = n_rwkv = 0
    for i in range(DEPTH):
        m = modt[i]
        kind = i % 3
        if kind == 0:
            j = n_attn
            qkv = _mod_proj(lay, x, m[0], m[1], attn_w_qkv[j], jnp.zeros((Q_DIM + 2 * KV_DIM,), f32), name="attn_qkv")
            past = cache_attn_k.shape[2]
            o_p = _context_attention(lay, qkv, attn_sink[j])
            o_s = _latent_attention(lay, qkv, cache_attn_k[:, j].reshape(bs, past, KV_DIM),
                                    cache_attn_v[:, j].reshape(bs, past, KV_DIM), attn_sink[j])
            y = jnp.concatenate([o_p, o_s], axis=0)
            w_out, b_out = attn_w_o[j], zero_b
            new_k.append(qkv[:lay.n_prompt, Q_DIM:Q_DIM + KV_DIM].reshape(bp, sp, N_KV_HEADS, HEAD_DIM))
            new_v.append(qkv[:lay.n_prompt, Q_DIM + KV_DIM:].reshape(bp, sp, N_KV_HEADS, HEAD_DIM))
            n_attn += 1
        elif kind == 1:
            j = n_conv
            y = _conformer(lay, x, m, conv_w_pw1[j], conv_b_pw1[j], conv_w_dw[j], conv_b_dw[j], conv_ln_g[j],
                           conv_ln_b[j], conv_w_pw2[j], conv_b_pw2[j])
            w_out, b_out = conv_w_pw2[j], conv_b_pw2[j]
            n_conv += 1
        else:
            j = n_rwkv
            y, sf, sb = _rwkv(lay, x, m, state_rwkv_fwd[:, j], state_rwkv_bwd[:, j], rwkv_mu[j], rwkv_w_rkv[j],
                              rwkv_w0[j], rwkv_w1[j], rwkv_w2[j], rwkv_a0[j], rwkv_a1[j], rwkv_a2[j], rwkv_g1[j],
                              rwkv_g2[j], rwkv_k_k[j], rwkv_k_a[j], rwkv_r_k[j], rwkv_gn_g[j], rwkv_gn_b[j])
            w_out, b_out = rwkv_w_o[j], zero_b
            new_sf.append(sf)
            new_sb.append(sb)
            n_rwkv += 1
        x = _out_proj_ln(lay, y, w_out, b_out, x, m[2], post_ln_g[i, 0], post_ln_b[i, 0], name="mixer_out_ln")
        x = _moe_layer(lay, x, m, moe_w_group[i], moe_b_group[i], moe_w_expert[i], moe_b_expert[i],
                       moe_w_gate_up[i], moe_w_down[i], post_ln_g[i, 1], post_ln_b[i, 1])
    y_prompt = x[:lay.n_prompt].reshape(bp, sp, D)
    y_sample = x[lay.n_prompt:].reshape(bs, ss, D)
    return (y_prompt, y_sample, jnp.stack(new_k, axis=1), jnp.stack(new_v, axis=1),
            jnp.stack(new_sf, axis=1).astype(x_prompt.dtype), jnp.stack(new_sb, axis=1).astype(x_prompt.dtype))
```

```python
import functools

import jax
import jax.numpy as jnp
from jax import lax
from jax.experimental import pallas as pl
from jax.experimental.pallas import tpu as pltpu

f32 = jnp.float32
bf16 = jnp.bfloat16

D = 1024
DEPTH = 4
GRID_W = 64
HEAD_DIM = 64
N_Q_HEADS = D // HEAD_DIM
N_KV_HEADS = N_Q_HEADS // 4
Q_PER_KV = N_Q_HEADS // N_KV_HEADS
Q_DIM = N_Q_HEADS * HEAD_DIM
KV_DIM = N_KV_HEADS * HEAD_DIM
WINDOW = 128
ROPE_BASE = 10000.0
CONV_WIDTH = 31
CONV_PAD = (CONV_WIDTH - 1) // 2
RWKV_HEAD = 64
RWKV_HEADS = D // RWKV_HEAD
GN_EPS = 64e-5
N_GROUPS = 4
EXPERTS_PER_GROUP = 8
N_EXPERTS = N_GROUPS * EXPERTS_PER_GROUP
TOP_K = 2
EXPERT_FF = D // 2
LN_EPS = 1e-5
ALPHA = (2 * DEPTH) ** 0.25

LANES = 128
ROW_TILE = 256
MOE_ROWS = 512
ATT_Q = 128
SCAN_CHUNK = 64
SCAN_ROWS = 256
SCAN_PAIRS = 4
MOD_ROWS = 16
VMEM_LIMIT = 56 * 1024 * 1024
NEG = -1e30
HI = lax.Precision.HIGHEST


def _params(n_axes):
    return pltpu.CompilerParams(dimension_semantics=("arbitrary",) * n_axes,
                                vmem_limit_bytes=VMEM_LIMIT)


def _const_spec(shape):
    nd = len(shape)
    return pl.BlockSpec(shape, lambda *_: (0,) * nd)


class _Layout:
    def __init__(self, bp, sp, bs, ss):
        self.bp, self.sp, self.bs, self.ss = bp, sp, bs, ss
        self.n_prompt = bp * sp
        self.n = bp * sp + bs * ss
        assert sp % ROW_TILE == 0 and ss % ROW_TILE == 0
        self.n_tiles = self.n // ROW_TILE
        self.prompt_tiles = self.n_prompt // ROW_TILE
        self.tiles_per_sseq = ss // ROW_TILE
        self.tiles_per_pseq = sp // ROW_TILE

    def mod_row(self, i):
        return jnp.where(i < self.prompt_tiles, 0, 1 + (i - self.prompt_tiles) // self.tiles_per_sseq)

    def seq_pos(self, i):
        in_p = i < self.prompt_tiles
        pos = jnp.where(in_p, i % self.tiles_per_pseq, (i - self.prompt_tiles) % self.tiles_per_sseq)
        cnt = jnp.where(in_p, self.tiles_per_pseq, self.tiles_per_sseq)
        return pos, cnt

    def row_spec(self, width=D, col=0):
        return pl.BlockSpec((ROW_TILE, width), lambda i: (i, col))

    def mod_spec(self):
        return pl.BlockSpec((None, 1, D), lambda i: (self.mod_row(i), 0, 0))


def _layer_norm(z, g, b, eps=LN_EPS):
    mu = jnp.mean(z, axis=-1, keepdims=True)
    zc = z - mu
    var = jnp.mean(zc * zc, axis=-1, keepdims=True)
    return zc * lax.rsqrt(var + eps) * g + b


def _silu(x):
    return x * jax.nn.sigmoid(x)


def _mod_body(c_ref, w_ref, b_ref, o_ref):
    o_ref[...] = jnp.dot(_silu(c_ref[...]), w_ref[...], precision=HI,
                         preferred_element_type=f32) + b_ref[...]


def _modulation_table(cond, mod_w, mod_b):
    out = pl.pallas_call(
        _mod_body,
        grid=(DEPTH, 6),
        in_specs=[_const_spec((MOD_ROWS, D)),
                  pl.BlockSpec((None, D, D), lambda l, j: (l, 0, j)),
                  pl.BlockSpec((None, 1, D), lambda l, j: (l, 0, j))],
        out_specs=pl.BlockSpec((None, None, MOD_ROWS, D), lambda l, j: (l, j, 0, 0)),
        out_shape=jax.ShapeDtypeStruct((DEPTH, 6, MOD_ROWS, D), f32),
        compiler_params=_params(2),
        name="modulation",
    )(cond, mod_w, mod_b.reshape(DEPTH, 1, 6 * D))
    return out.reshape(DEPTH, 6, MOD_ROWS, 1, D)


def _proj_body(x_ref, sh_ref, sc_ref, w_ref, b_ref, o_ref, *, glu):
    h = x_ref[...] * (1.0 + sc_ref[...]) + sh_ref[...]
    u = jnp.dot(h.astype(bf16), w_ref[...], preferred_element_type=f32) + b_ref[...]
    if glu:
        half = u.shape[1] // 2
        u = u[:, :half] * jax.nn.sigmoid(u[:, half:])
    o_ref[...] = u.astype(o_ref.dtype)


def _mod_proj(lay, x, shift, scale, w, b, *, glu=False, out_dtype=f32, name="proj"):
    k, n = w.shape
    n_out = n // 2 if glu else n
    return pl.pallas_call(
        functools.partial(_proj_body, glu=glu),
        grid=(lay.n_tiles,),
        in_specs=[lay.row_spec(), lay.mod_spec(), lay.mod_spec(),
                  _const_spec((k, n)), _const_spec((1, n))],
        out_specs=lay.row_spec(n_out),
        out_shape=jax.ShapeDtypeStruct((lay.n, n_out), out_dtype),
        compiler_params=_params(1),
        name=name,
    )(x, shift, scale, w.astype(bf16), b.reshape(1, n))


def _out_ln_body(y_ref, w_ref, b_ref, x_ref, gate_ref, g_ref, be_ref, o_ref):
    t = jnp.dot(y_ref[...].astype(bf16), w_ref[...], preferred_element_type=f32) + b_ref[...]
    z = ALPHA * x_ref[...] + gate_ref[...] * t
    o_ref[...] = _layer_norm(z, g_ref[...], be_ref[...])


def _out_proj_ln(lay, y, w, b, x, gate, ln_g, ln_b, name="out_ln"):
    k = w.shape[0]
    return pl.pallas_call(
        _out_ln_body,
        grid=(lay.n_tiles,),
        in_specs=[lay.row_spec(k), _const_spec((k, D)), _const_spec((1, D)),
                  lay.row_spec(), lay.mod_spec(), _const_spec((1, D)), _const_spec((1, D))],
        out_specs=lay.row_spec(),
        out_shape=jax.ShapeDtypeStruct((lay.n, D), f32),
        compiler_params=_params(1),
        name=name,
    )(y, w.astype(bf16), b.reshape(1, D), x, gate, ln_g.reshape(1, D), ln_b.reshape(1, D))


def _sink_column(sink_ref, h, rows_per_head):
    r = lax.broadcasted_iota(jnp.int32, (Q_PER_KV * rows_per_head, 1), 0) // rows_per_head
    col = jnp.full(r.shape, sink_ref[h * Q_PER_KV], f32)
    for g in range(1, Q_PER_KV):
        col = jnp.where(r == g, sink_ref[h * Q_PER_KV + g], col)
    return col


def _stack_heads(q, h):
    return jnp.concatenate(
        [q[:, (h * Q_PER_KV + g) * HEAD_DIM:(h * Q_PER_KV + g + 1) * HEAD_DIM] for g in range(Q_PER_KV)], axis=0)


def _nt(a, b):
    return lax.dot_general(a, b, (((1,), (1,)), ((), ())), preferred_element_type=f32)


def _ctx_attn_body(sink_ref, q_ref, k_ref, v_ref, o_ref):
    rows = q_ref.shape[0]
    q = q_ref[...] * (HEAD_DIM ** -0.5)
    k = k_ref[...]
    v = v_ref[...]
    outs = [None] * N_Q_HEADS
    for h in range(N_KV_HEADS):
        qh = _stack_heads(q, h).astype(bf16)
        kh = k[:, h * HEAD_DIM:(h + 1) * HEAD_DIM].astype(bf16)
        vh = v[:, h * HEAD_DIM:(h + 1) * HEAD_DIM].astype(bf16)
        s = _nt(qh, kh)
        sk = _sink_column(sink_ref, h, rows)
        m = jnp.maximum(jnp.max(s, axis=-1, keepdims=True), sk)
        p = jnp.exp(s - m)
        den = jnp.sum(p, axis=-1, keepdims=True) + jnp.exp(sk - m)
        o = jnp.dot(p.astype(bf16), vh, preferred_element_type=f32) / den
        for g in range(Q_PER_KV):
            outs[h * Q_PER_KV + g] = o[g * rows:(g + 1) * rows]
    o_ref[...] = jnp.concatenate(outs, axis=1).astype(o_ref.dtype)


def _context_attention(lay, qkv, sink):
    sp = lay.sp
    kcol = Q_DIM // KV_DIM
    return pl.pallas_call(
        _ctx_attn_body,
        grid=(lay.bp,),
        in_specs=[pl.BlockSpec(memory_space=pltpu.SMEM),
                  pl.BlockSpec((sp, Q_DIM), lambda b: (b, 0)),
                  pl.BlockSpec((sp, KV_DIM), lambda b: (b, kcol)),
                  pl.BlockSpec((sp, KV_DIM), lambda b: (b, kcol + 1))],
        out_specs=pl.BlockSpec((sp, Q_DIM), lambda b: (b, 0)),
        out_shape=jax.ShapeDtypeStruct((lay.n_prompt, Q_DIM), bf16),
        compiler_params=_params(1),
        name="ctx_attn",
    )(sink, qkv, qkv, qkv)


def _rope_tables(t_len):
    t = jnp.arange(t_len)
    row = (t // GRID_W).astype(f32)
    col = (t % GRID_W).astype(f32)
    quarter = HEAD_DIM // 4
    d = jnp.arange(LANES) % HEAD_DIM
    inv = ROPE_BASE ** (-(d % quarter).astype(f32) / quarter)
    pos = jnp.where((d < HEAD_DIM // 2)[None, :], row[:, None], col[:, None])
    ang = pos * inv[None, :]
    cs, sn = jnp.cos(ang), jnp.sin(ang)
    first = ((d % (2 * quarter)) < quarter)[None, :]
    return cs, jnp.where(first, -sn, 0.0), jnp.where(first, 0.0, sn)


def _rope(x, cs, sa, sb):
    quarter = HEAD_DIM // 4
    outs = []
    for c in range(x.shape[1] // LANES):
        xc = x[:, c * LANES:(c + 1) * LANES]
        outs.append(xc * cs + pltpu.roll(xc, LANES - quarter, axis=1) * sa + pltpu.roll(xc, quarter, axis=1) * sb)
    return outs[0] if len(outs) == 1 else jnp.concatenate(outs, axis=1)


def _lat_attn_body(sink_ref, q_ref, k_ref, v_ref, kc_ref, vc_ref, cs_ref, sa_ref, sb_ref, o_ref):
    j = pl.program_id(1)
    t_len = k_ref.shape[0]
    win = 3 * ATT_Q
    start = pl.multiple_of(jnp.clip((j - 1) * ATT_Q, 0, t_len - win), ATT_Q)
    q0 = pl.multiple_of(j * ATT_Q, ATT_Q)
    q = _rope(q_ref[...], cs_ref[pl.ds(q0, ATT_Q), :], sa_ref[pl.ds(q0, ATT_Q), :],
              sb_ref[pl.ds(q0, ATT_Q), :]) * (HEAD_DIM ** -0.5)
    kw = _rope(k_ref[pl.ds(start, win), :], cs_ref[pl.ds(start, win), :], sa_ref[pl.ds(start, win), :],
               sb_ref[pl.ds(start, win), :])
    vw = v_ref[pl.ds(start, win), :]
    kc = kc_ref[...]
    vc = vc_ref[...]
    rows = Q_PER_KV * ATT_Q
    qpos = q0 + lax.broadcasted_iota(jnp.int32, (rows, win), 0) % ATT_Q
    kpos = start + lax.broadcasted_iota(jnp.int32, (rows, win), 1)
    valid = jnp.abs(qpos - kpos) <= WINDOW
    outs = [None] * N_Q_HEADS
    for h in range(N_KV_HEADS):
        sl = slice(h * HEAD_DIM, (h + 1) * HEAD_DIM)
        qh = _stack_heads(q, h).astype(bf16)
        s_c = _nt(qh, kc[:, sl].astype(bf16))
        s_l = jnp.where(valid, _nt(qh, kw[:, sl].astype(bf16)), NEG)
        sk = _sink_column(sink_ref, h, ATT_Q)
        m = jnp.maximum(jnp.maximum(jnp.max(s_c, axis=-1, keepdims=True),
                                    jnp.max(s_l, axis=-1, keepdims=True)), sk)
        p_c = jnp.exp(s_c - m)
        p_l = jnp.exp(s_l - m)
        den = jnp.sum(p_c, axis=-1, keepdims=True) + jnp.sum(p_l, axis=-1, keepdims=True) + jnp.exp(sk - m)
        o = (jnp.dot(p_c.astype(bf16), vc[:, sl].astype(bf16), preferred_element_type=f32)
             + jnp.dot(p_l.astype(bf16), vw[:, sl].astype(bf16), preferred_element_type=f32)) / den
        for g in range(Q_PER_KV):
            outs[h * Q_PER_KV + g] = o[g * ATT_Q:(g + 1) * ATT_Q]
    o_ref[...] = jnp.concatenate(outs, axis=1).astype(o_ref.dtype)


def _latent_attention(lay, qkv, kc, vc, sink):
    ss = lay.ss
    past = kc.shape[1]
    nq = ss // ATT_Q
    q_base = lay.n_prompt // ATT_Q
    s_base = lay.n_prompt // ss
    assert lay.n_prompt % ss == 0
    kcol = Q_DIM // KV_DIM
    cs, sa, sb = _rope_tables(ss)
    return pl.pallas_call(
        _lat_attn_body,
        grid=(lay.bs, nq),
        in_specs=[pl.BlockSpec(memory_space=pltpu.SMEM),
                  pl.BlockSpec((ATT_Q, Q_DIM), lambda b, j: (q_base + b * nq + j, 0)),
                  pl.BlockSpec((ss, KV_DIM), lambda b, j: (s_base + b, kcol)),
                  pl.BlockSpec((ss, KV_DIM), lambda b, j: (s_base + b, kcol + 1)),
                  pl.BlockSpec((None, past, KV_DIM), lambda b, j: (b, 0, 0)),
                  pl.BlockSpec((None, past, KV_DIM), lambda b, j: (b, 0, 0)),
                  _const_spec((ss, LANES)), _const_spec((ss, LANES)), _const_spec((ss, LANES))],
        out_specs=pl.BlockSpec((ATT_Q, Q_DIM), lambda b, j: (b * nq + j, 0)),
        out_shape=jax.ShapeDtypeStruct((lay.bs * ss, Q_DIM), bf16),
        compiler_params=_params(2),
        name="lat_attn",
    )(sink, qkv, qkv, qkv, kc, vc, cs, sa, sb)


CONV_HALO = 16
CONV_SUB = 32


def _dwconv_body(lay, u_ref, prev_ref, next_ref, w_ref, b_ref, g_ref, be_ref, o_ref, buf_ref):
    i = pl.program_id(0)
    pos, cnt = lay.seq_pos(i)
    buf_ref[pl.ds(0, CONV_HALO), :] = jnp.where(pos > 0, prev_ref[...], 0.0)
    buf_ref[pl.ds(CONV_HALO, ROW_TILE), :] = u_ref[...]
    buf_ref[pl.ds(CONV_HALO + ROW_TILE, CONV_HALO), :] = jnp.where(pos < cnt - 1, next_ref[...], 0.0)
    w = w_ref[...]
    off = CONV_HALO - CONV_PAD

    for s in range(ROW_TILE // CONV_SUB):
        r0 = s * CONV_SUB
        acc = jnp.zeros((CONV_SUB, D), f32)
        for t in range(CONV_WIDTH):
            acc = acc + buf_ref[pl.ds(r0 + off + t, CONV_SUB), :] * w[t:t + 1, :]
        z = _layer_norm(acc + b_ref[...], g_ref[...], be_ref[...])
        o_ref[pl.ds(r0, CONV_SUB), :] = _silu(z).astype(o_ref.dtype)


def _dwconv_ln_swish(lay, u, w_dw, b_dw, ln_g, ln_b):
    hpt = ROW_TILE // CONV_HALO
    last = lay.n // CONV_HALO - 1
    return pl.pallas_call(
        functools.partial(_dwconv_body, lay),
        grid=(lay.n_tiles,),
        in_specs=[lay.row_spec(),
                  pl.BlockSpec((CONV_HALO, D), lambda i: (jnp.maximum(i * hpt - 1, 0), 0)),
                  pl.BlockSpec((CONV_HALO, D), lambda i: (jnp.minimum((i + 1) * hpt, last), 0)),
                  _const_spec((CONV_WIDTH, D)), _const_spec((1, D)), _const_spec((1, D)), _const_spec((1, D))],
        out_specs=lay.row_spec(),
        out_shape=jax.ShapeDtypeStruct((lay.n, D), bf16),
        scratch_shapes=[pltpu.VMEM((ROW_TILE + 2 * CONV_HALO, D), f32)],
        compiler_params=_params(1),
        name="dwconv",
    )(u, u, u, w_dw, b_dw.reshape(1, D), ln_g.reshape(1, D), ln_b.reshape(1, D))


def _conformer(lay, x, m, w_pw1, b_pw1, w_dw, b_dw, ln_g, ln_b, w_pw2, b_pw2):
    u = _mod_proj(lay, x, m[0], m[1], w_pw1, b_pw1, glu=True, name="conv_pw1_glu")
    return _dwconv_ln_swish(lay, u, w_dw, b_dw, ln_g, ln_b)


LORA_PAD = 128
GATE_PAD = 256


def _bdot(a, w):
    return jnp.dot(a.astype(bf16), w, preferred_element_type=f32)


def _rwkv_proj_body(lay, x_ref, prev_ref, next_ref, sh_ref, sc_ref, mu_ref, wrkv_ref, w0_ref, w1_ref, w2_ref,
                    a0_ref, a1_ref, a2_ref, g1_ref, g2_ref, r_ref, k_ref, v_ref, g_ref, lw_ref, a_ref):
    i = pl.program_id(0)
    pos, cnt = lay.seq_pos(i)
    sc = 1.0 + sc_ref[...]
    sh = sh_ref[...]
    h = x_ref[...] * sc + sh
    hb = prev_ref.shape[0]
    h_prev = jnp.where(pos > 0, prev_ref[hb - 1:hb, :] * sc + sh, 0.0)
    h_next = jnp.where(pos < cnt - 1, next_ref[0:1, :] * sc + sh, 0.0)
    row = lax.broadcasted_iota(jnp.int32, (ROW_TILE, 1), 0)
    below = jnp.where(row == 0, h_prev, pltpu.roll(h, 1, axis=0))
    above = jnp.where(row == ROW_TILE - 1, h_next, pltpu.roll(h, ROW_TILE - 1, axis=0))
    xx = 0.5 * (below + above) - h

    def mix(j):
        return (h + xx * mu_ref[j:j + 1, :]).astype(bf16)

    r_ref[...] = _bdot(mix(0), wrkv_ref[0])
    k_ref[...] = _bdot(mix(2), wrkv_ref[1])
    v_ref[...] = _bdot(mix(3), wrkv_ref[2])
    g_ref[...] = _bdot(jax.nn.sigmoid(_bdot(mix(5), g1_ref[...])), g2_ref[...])
    xw = mix(1)
    xa = mix(4)
    for d in range(2):
        z = w0_ref[d:d + 1, :] + _bdot(jnp.tanh(_bdot(xw, w1_ref[d])), w2_ref[d])
        softplus = jnp.maximum(-z, 0.0) + jnp.log(1.0 + jnp.exp(-jnp.abs(z)))
        lw_ref[d] = -jnp.exp(-softplus - 0.5)
        a_ref[d] = jax.nn.sigmoid(a0_ref[d:d + 1, :] + _bdot(_bdot(xa, a1_ref[d]), a2_ref[d]))


def _pad_to(w, axis, size):
    pad = [(0, 0)] * w.ndim
    pad[axis] = (0, size - w.shape[axis])
    return jnp.pad(w, pad)


def _rwkv_project(lay, x, m, mu, w_rkv, w0, w1, w2, a0, a1, a2, g1, g2):
    halo = 8
    hpt = ROW_TILE // halo
    last = lay.n // halo - 1
    nd = (lay.n, D)
    consts = [mu, w_rkv.astype(bf16), w0,
              _pad_to(w1, 2, LORA_PAD).astype(bf16), _pad_to(w2, 1, LORA_PAD).astype(bf16), a0,
              _pad_to(a1, 2, LORA_PAD).astype(bf16), _pad_to(a2, 1, LORA_PAD).astype(bf16),
              _pad_to(g1, 1, GATE_PAD).astype(bf16), _pad_to(g2, 0, GATE_PAD).astype(bf16)]
    return pl.pallas_call(
        functools.partial(_rwkv_proj_body, lay),
        grid=(lay.n_tiles,),
        in_specs=[lay.row_spec(),
                  pl.BlockSpec((halo, D), lambda i: (jnp.maximum(i * hpt - 1, 0), 0)),
                  pl.BlockSpec((halo, D), lambda i: (jnp.minimum((i + 1) * hpt, last), 0)),
                  lay.mod_spec(), lay.mod_spec()] + [_const_spec(c.shape) for c in consts],
        out_specs=[lay.row_spec()] * 4 + [pl.BlockSpec((2, ROW_TILE, D), lambda i: (0, i, 0))] * 2,
        out_shape=[jax.ShapeDtypeStruct(nd, f32)] * 4 + [jax.ShapeDtypeStruct((2,) + nd, f32)] * 2,
        compiler_params=_params(1),
        name="rwkv_proj",
    )(x, x, x, m[0], m[1], *consts)


def _split_dot(x, c, parts):
    acc = None
    for _ in range(parts):
        hi = x.astype(bf16)
        t = jnp.dot(hi, c, preferred_element_type=f32)
        acc = t if acc is None else acc + t
        x = x - hi.astype(f32)
    return acc


def _split_dot_left(c, x, parts):
    acc = None
    for _ in range(parts):
        hi = x.astype(bf16)
        t = jnp.dot(c, hi, preferred_element_type=f32)
        acc = t if acc is None else acc + t
        x = x - hi.astype(f32)
    return acc


def _mm(a, b):
    return jnp.dot(a.astype(bf16), b.astype(bf16), preferred_element_type=f32)


def _split3(x):
    x1 = x.astype(bf16)
    r1 = x - x1.astype(f32)
    x2 = r1.astype(bf16)
    x3 = (r1 - x2.astype(f32)).astype(bf16)
    return x1, x2, x3


def _mm_f32(a3, b):
    b1, b2, b3 = _split3(b)
    n = a3[0].shape[0]
    r1 = jnp.dot(jnp.concatenate(a3, axis=0), b1, preferred_element_type=f32)
    r2 = jnp.dot(jnp.concatenate(a3[:2], axis=0), b2, preferred_element_type=f32)
    r3 = jnp.dot(a3[0], b3, preferred_element_type=f32)
    small = r1[2 * n:] + r2[n:] + r3
    return r1[:n] + ((r1[n:2 * n] + r2[:n]) + small)


def _mm_nt(a, b):
    return lax.dot_general(a.astype(bf16), b.astype(bf16), (((1,), (1,)), ((), ())), preferred_element_type=f32)


def _scan_body(lay, rev, r_ref, k_ref, v_ref, lw_ref, a_ref, kk_ref, ka_ref, rk_ref, s0_ref, tri_ref, bd_ref,
               y_ref, bon_ref, sfin_ref, state_ref):
    t_idx = pl.program_id(1)
    i = (lay.n_tiles - 1 - t_idx) if rev else t_idx
    pos, cnt = lay.seq_pos(i)
    first = (pos == cnt - 1) if rev else (pos == 0)
    final = (pos == 0) if rev else (pos == cnt - 1)
    c_len = SCAN_CHUNK
    n_chunks = SCAN_ROWS // c_len

    @pl.when(first)
    def _():
        state_ref[...] = s0_ref[...]

    lane = lax.broadcasted_iota(jnp.int32, (1, LANES), 1)
    head0 = lane < RWKV_HEAD
    rr = lax.broadcasted_iota(jnp.int32, (2 * c_len, 2 * c_len), 0)
    cc = lax.broadcasted_iota(jnp.int32, (2 * c_len, 2 * c_len), 1)
    same = (rr // c_len) == (cc // c_len)
    tt = rr % c_len
    ss = cc % c_len
    strict = same & ((tt < ss) if rev else (tt > ss))
    incl = same & ((tt <= ss) if rev else (tt >= ss))
    tri = tri_ref[...]
    bd = bd_ref[...]

    def stack(xc):
        return jnp.concatenate([jnp.where(head0, xc, 0.0), jnp.where(head0, 0.0, xc)], axis=0)

    def pair_chunk(rows, p):
        cols = slice(p * LANES, (p + 1) * LANES)
        r = r_ref[rows, cols]
        k = k_ref[rows, cols]
        v = v_ref[rows, cols]
        lw = lw_ref[rows, cols]
        a = a_ref[rows, cols]
        kk = k * kk_ref[:, cols]
        nrm = jnp.sqrt(_split_dot(kk * kk, bd, 3))
        kk = kk / jnp.maximum(nrm, 1e-12)
        b = kk * a
        kd = k * (1.0 + (a - 1.0) * ka_ref[:, cols])
        bon_ref[rows, cols] = _split_dot(r * kd * rk_ref[:, cols], bd, 3) * v
        cum = _split_dot_left(tri, lw, 3)
        tot = cum[0:1, :] if rev else cum[c_len - 1:c_len, :]
        e_tot = jnp.exp(tot)
        e_neg = jnp.exp(-cum)
        a_t = -kk * jnp.exp(cum - lw)
        r_t = r * jnp.exp(cum)
        b_t = b * e_neg
        k_t = kd * e_neg
        ar = jnp.concatenate([stack(a_t), stack(r_t)], axis=0)
        bk = jnp.concatenate([stack(b_t), stack(k_t)], axis=0)
        m = _mm_nt(ar, bk)
        h2 = 2 * c_len
        n_ab = jnp.where(strict, m[:h2, :h2], 0.0)
        a_ak = jnp.where(strict, m[:h2, h2:], 0.0)
        a_rb = jnp.where(incl, m[h2:, :h2], 0.0)
        a_rk = jnp.where(incl, m[h2:, h2:], 0.0)
        state = state_ref[p]
        x0 = _mm_nt(ar, state)
        v_st = stack(v)
        sa = x0[:h2] + _mm(a_ak, v_st)
        pw = n_ab
        span = 1
        while span < c_len:
            pw3 = _split3(pw)
            sa = sa + _mm_f32(pw3, sa)
            span *= 2
            if span < c_len:
                pw = _mm_f32(pw3, pw)
        sav = jnp.concatenate([sa, v_st], axis=0)
        y_st = x0[h2:] + _mm(jnp.concatenate([a_rb, a_rk], axis=1), sav)
        y_ref[rows, cols] = y_st[:c_len] + y_st[c_len:]
        state_ref[p] = state * e_tot + _mm(sav.T, bk * e_tot)

    def chunk(c, carry):
        ce = (n_chunks - 1 - c) if rev else c
        rows = pl.ds(pl.multiple_of(ce * c_len, c_len), c_len)
        for p in range(SCAN_PAIRS):
            pair_chunk(rows, p)
        return carry

    lax.fori_loop(0, n_chunks, chunk, 0)

    @pl.when(final)
    def _():
        sfin_ref[...] = state_ref[...]


def _block_diag_states(s):
    b = s.shape[0]
    s = s.reshape(b, RWKV_HEADS // 2, 2, RWKV_HEAD, RWKV_HEAD)
    eye = jnp.eye(2, dtype=s.dtype)
    out = s[:, :, :, :, None, :] * eye[None, None, :, None, :, None]
    return out.reshape(b, RWKV_HEADS // 2, LANES, LANES)


def _diag_blocks(w):
    b = w.shape[0]
    w = w.reshape(b, RWKV_HEADS // 2, 2, RWKV_HEAD, 2, RWKV_HEAD)
    return jnp.stack([w[:, :, 0, :, 0, :], w[:, :, 1, :, 1, :]], axis=2).reshape(b, RWKV_HEADS, RWKV_HEAD, RWKV_HEAD)


def _rwkv_scan(lay, rev, r, k, v, lw, a, k_k, k_a, r_k, s0):
    d = 1 if rev else 0
    n_seq = lay.bp + lay.bs
    c_len = SCAN_CHUNK
    idx = jnp.arange(c_len)
    tri = ((idx[:, None] <= idx[None, :]) if rev else (idx[:, None] >= idx[None, :])).astype(bf16)
    hh = jnp.arange(LANES) // RWKV_HEAD
    bd = (hh[:, None] == hh[None, :]).astype(bf16)

    def tile(t):
        return (lay.n_tiles - 1 - t) if rev else t

    def seq(t):
        i = tile(t)
        return jnp.where(i < lay.prompt_tiles, i // lay.tiles_per_pseq,
                         lay.bp + (i - lay.prompt_tiles) // lay.tiles_per_sseq)

    width = SCAN_PAIRS * LANES
    row = pl.BlockSpec((SCAN_ROWS, width), lambda p, t: (tile(t), p))
    row_d = pl.BlockSpec((None, SCAN_ROWS, width), lambda p, t: (d, tile(t), p))
    par = pl.BlockSpec((1, width), lambda p, t: (0, p))
    st = pl.BlockSpec((None, SCAN_PAIRS, LANES, LANES), lambda p, t: (seq(t), p, 0, 0))
    return pl.pallas_call(
        functools.partial(_scan_body, lay, rev),
        grid=(RWKV_HEADS // 2 // SCAN_PAIRS, lay.n_tiles),
        in_specs=[row, row, row, row_d, row_d, par, par, par, st,
                  _const_spec((c_len, c_len)), _const_spec((LANES, LANES))],
        out_specs=[row, row, st],
        out_shape=[jax.ShapeDtypeStruct((lay.n, D), f32), jax.ShapeDtypeStruct((lay.n, D), f32),
                   jax.ShapeDtypeStruct((n_seq, RWKV_HEADS // 2, LANES, LANES), f32)],
        scratch_shapes=[pltpu.VMEM((SCAN_PAIRS, LANES, LANES), f32)],
        compiler_params=_params(2),
        name="rwkv_scan_bwd" if rev else "rwkv_scan_fwd",
    )(r, k, v, lw, a, k_k.reshape(1, D), k_a.reshape(1, D), r_k.reshape(1, D), s0, tri, bd)


def _rwkv_out_body(yf_ref, yb_ref, bf_ref, bb_ref, g_ref, gg_ref, gb_ref, bd_ref, o_ref):
    y = yf_ref[...] + yb_ref[...]
    bd = bd_ref[...]
    inv = 1.0 / RWKV_HEAD
    outs = []
    for c in range(D // LANES):
        yc = y[:, c * LANES:(c + 1) * LANES]
        cen = yc - _split_dot(yc, bd, 3) * inv
        var = _split_dot(cen * cen, bd, 3) * inv
        outs.append(cen * lax.rsqrt(var + GN_EPS))
    yn = jnp.concatenate(outs, axis=1) * gg_ref[...] + gb_ref[...]
    o_ref[...] = ((yn + bf_ref[...] + bb_ref[...]) * g_ref[...]).astype(o_ref.dtype)


def _rwkv(lay, x, m, s0_f, s0_b, mu, w_rkv, w0, w1, w2, a0, a1, a2, g1, g2, k_k, k_a, r_k, gn_g, gn_b):
    r, k, v, g, lw, a = _rwkv_project(lay, x, m, mu, w_rkv, w0, w1, w2, a0, a1, a2, g1, g2)
    zeros = jnp.zeros((lay.bp, RWKV_HEADS // 2, LANES, LANES), f32)
    outs = []
    for rev, s0 in ((False, s0_f), (True, s0_b)):
        s_all = jnp.concatenate([zeros, _block_diag_states(s0.astype(f32))], axis=0)
        outs.append(_rwkv_scan(lay, rev, r, k, v, lw, a, k_k, k_a, r_k, s_all))
    hh = jnp.arange(LANES) // RWKV_HEAD
    bd = (hh[:, None] == hh[None, :]).astype(bf16)
    y = pl.pallas_call(
        _rwkv_out_body,
        grid=(lay.n_tiles,),
        in_specs=[lay.row_spec()] * 5 + [_const_spec((1, D)), _const_spec((1, D)), _const_spec((LANES, LANES))],
        out_specs=lay.row_spec(),
        out_shape=jax.ShapeDtypeStruct((lay.n, D), bf16),
        compiler_params=_params(1),
        name="rwkv_groupnorm_gate",
    )(outs[0][0], outs[1][0], outs[0][1], outs[1][1], g, gn_g.reshape(1, D), gn_b.reshape(1, D), bd)
    return y, _diag_blocks(outs[0][2][:lay.bp]), _diag_blocks(outs[1][2][:lay.bp])


def _router_body(x_ref, sh_ref, sc_ref, w_ref, b_ref, h_ref, route_ref):
    h = x_ref[...] * (1.0 + sc_ref[...]) + sh_ref[...]
    h_ref[...] = h.astype(h_ref.dtype)
    lg = jnp.dot(h, w_ref[...], precision=HI, preferred_element_type=f32) + b_ref[...]
    lane = lax.broadcasted_iota(jnp.int32, lg.shape, 1)

    def top1(vals):
        mx = jnp.max(vals, axis=-1, keepdims=True)
        idx = jnp.min(jnp.where(vals == mx, lane, LANES), axis=-1, keepdims=True)
        return mx, idx

    gl = jnp.where(lane < N_GROUPS, lg, NEG)
    g_max, g_idx = top1(gl)
    p_g = 1.0 / jnp.sum(jnp.exp(gl - g_max), axis=-1, keepdims=True)
    lo = N_GROUPS + g_idx * EXPERTS_PER_GROUP
    el = jnp.where((lane >= lo) & (lane < lo + EXPERTS_PER_GROUP), lg, NEG)
    m1, i1 = top1(el)
    m2, i2 = top1(jnp.where(lane == i1, NEG, el))
    e2 = jnp.exp(m2 - m1)
    w1 = p_g / (1.0 + e2)
    w2 = p_g * e2 / (1.0 + e2)
    route = jnp.where(lane == 0, (i1 - N_GROUPS).astype(f32),
                      jnp.where(lane == 1, (i2 - N_GROUPS).astype(f32),
                                jnp.where(lane == 2, w1, jnp.where(lane == 3, w2, 0.0))))
    route_ref[...] = route


def _expert_body(be_ref, nu_ref, x_ref, wgu_ref, wd_ref, o_ref, wgu_s, wd_s):
    i = pl.program_id(0)
    used = i < nu_ref[0]

    @pl.when(used & ((i == 0) | (be_ref[i] != be_ref[jnp.maximum(i - 1, 0)])))
    def _():
        wgu_s[...] = wgu_ref[...].astype(bf16)
        wd_s[...] = wd_ref[...].astype(bf16)

    @pl.when(used)
    def _():
        u = jnp.dot(x_ref[...], wgu_s[...], preferred_element_type=f32)
        act = _silu(u[:, :EXPERT_FF]) * u[:, EXPERT_FF:]
        o_ref[...] = jnp.dot(act.astype(bf16), wd_s[...], preferred_element_type=f32)

    @pl.when(jnp.logical_not(used))
    def _():
        o_ref[...] = jnp.zeros_like(o_ref)


def _moe_out_body(y_ref0, y_ref1, route_ref, x_ref, gate_ref, g_ref, be_ref, o_ref):
    route = route_ref[...]
    y = route[:, 2:3] * y_ref0[...] + route[:, 3:4] * y_ref1[...]
    z = ALPHA * x_ref[...] + gate_ref[...] * y
    o_ref[...] = _layer_norm(z, g_ref[...], be_ref[...])


def _moe_layer(lay, x, m, w_group, b_group, w_expert, b_expert, w_gu, w_down, ln_g, ln_b):
    n = lay.n
    n_route = N_GROUPS + N_EXPERTS
    w_r = _pad_to(jnp.concatenate([w_group, w_expert], axis=1).astype(f32), 1, LANES)
    b_r = _pad_to(jnp.concatenate([b_group, b_expert]).astype(f32).reshape(1, n_route), 1, LANES)
    h, route = pl.pallas_call(
        _router_body,
        grid=(lay.n_tiles,),
        in_specs=[lay.row_spec(), lay.mod_spec(), lay.mod_spec(), _const_spec((D, LANES)), _const_spec((1, LANES))],
        out_specs=[lay.row_spec(), lay.row_spec(LANES)],
        out_shape=[jax.ShapeDtypeStruct((n, D), bf16), jax.ShapeDtypeStruct((n, LANES), f32)],
        compiler_params=_params(1),
        name="moe_router",
    )(x, m[3], m[4], w_r, b_r)

    s = n * TOP_K
    e_flat = route[:, :TOP_K].astype(jnp.int32).reshape(-1)
    pair = jnp.arange(s, dtype=jnp.int32)
    e_sorted, order = lax.sort((e_flat, pair), num_keys=1, is_stable=True)
    experts = jnp.arange(N_EXPERTS, dtype=jnp.int32)
    counts = jnp.sum((e_flat[:, None] == experts[None, :]).astype(jnp.int32), axis=0)
    padded = (counts + MOE_ROWS - 1) // MOE_ROWS * MOE_ROWS
    pad_end = jnp.cumsum(padded)
    pad_start = pad_end - padded
    start = jnp.cumsum(counts) - counts
    dest_sorted = pad_start[e_sorted] + pair - start[e_sorted]
    n_blocks = -(-s // MOE_ROWS) + N_EXPERTS
    p_rows = n_blocks * MOE_ROWS
    block_start = jnp.arange(n_blocks, dtype=jnp.int32) * MOE_ROWS
    block_expert = jnp.minimum(jnp.sum((pad_end[None, :] <= block_start[:, None]).astype(jnp.int32), axis=1),
                               N_EXPERTS - 1)
    n_used = (pad_end[N_EXPERTS - 1] // MOE_ROWS).astype(jnp.int32).reshape(1)
    slot = jnp.arange(p_rows, dtype=jnp.int32)
    slot_expert = jnp.repeat(block_expert, MOE_ROWS)
    in_run = jnp.minimum(slot - pad_start[slot_expert], jnp.maximum(counts[slot_expert] - 1, 0))
    slot_token = order[jnp.clip(start[slot_expert] + in_run, 0, s - 1)] // TOP_K
    _, dest = lax.sort((order, dest_sorted), num_keys=1)
    dest = dest.reshape(n, TOP_K)

    xb = h.at[slot_token].get(mode="promise_in_bounds")
    yb = pl.pallas_call(
        _expert_body,
        grid_spec=pltpu.PrefetchScalarGridSpec(
            num_scalar_prefetch=2,
            grid=(n_blocks,),
            in_specs=[pl.BlockSpec((MOE_ROWS, D), lambda i, be, nu: (jnp.minimum(i, nu[0] - 1), 0)),
                      pl.BlockSpec((None, D, 2 * EXPERT_FF), lambda i, be, nu: (be[i], 0, 0)),
                      pl.BlockSpec((None, EXPERT_FF, D), lambda i, be, nu: (be[i], 0, 0))],
            out_specs=pl.BlockSpec((MOE_ROWS, D), lambda i, be, nu: (i, 0)),
            scratch_shapes=[pltpu.VMEM((D, 2 * EXPERT_FF), bf16), pltpu.VMEM((EXPERT_FF, D), bf16)]),
        out_shape=jax.ShapeDtypeStruct((p_rows, D), f32),
        compiler_params=_params(1),
        name="moe_experts",
    )(block_expert, n_used, xb, w_gu, w_down)
    y0 = yb.at[dest[:, 0]].get(mode="promise_in_bounds")
    y1 = yb.at[dest[:, 1]].get(mode="promise_in_bounds")

    return pl.pallas_call(
        _moe_out_body,
        grid=(lay.n_tiles,),
        in_specs=[lay.row_spec(), lay.row_spec(), lay.row_spec(LANES), lay.row_spec(), lay.mod_spec(),
                  _const_spec((1, D)), _const_spec((1, D))],
        out_specs=lay.row_spec(),
        out_shape=jax.ShapeDtypeStruct((n, D), f32),
        compiler_params=_params(1),
        name="moe_combine_ln",
    )(y0, y1, route, x, m[5], ln_g.reshape(1, D), ln_b.reshape(1, D))


def kernel(x_prompt, x_sample, c, c_ctx, cache_attn_k, cache_attn_v, state_rwkv_fwd, state_rwkv_bwd, mod_w, mod_b, post_ln_g, post_ln_b, attn_w_qkv, attn_w_o, attn_sink, conv_w_pw1, conv_b_pw1, conv_w_dw, conv_b_dw, conv_ln_g, conv_ln_b, conv_w_pw2, conv_b_pw2, rwkv_mu, rwkv_w_rkv, rwkv_w0, rwkv_w1, rwkv_w2, rwkv_a0, rwkv_a1, rwkv_a2, rwkv_g1, rwkv_g2, rwkv_k_k, rwkv_k_a, rwkv_r_k, rwkv_gn_g, rwkv_gn_b, rwkv_w_o, moe_w_group, moe_b_group, moe_w_expert, moe_b_expert, moe_w_gate_up, moe_w_down):
    bp, sp, _ = x_prompt.shape
    bs, ss, _ = x_sample.shape
    assert 1 + bs <= MOD_ROWS
    lay = _Layout(bp, sp, bs, ss)
    x = jnp.concatenate([x_prompt.reshape(bp * sp, D), x_sample.reshape(bs * ss, D)], axis=0)
    cond = jnp.concatenate([c_ctx[None, :], c, jnp.zeros((MOD_ROWS - 1 - bs, D), f32)], axis=0)
    modt = _modulation_table(cond, mod_w, mod_b)
    zero_b = jnp.zeros((D,), f32)
    new_k, new_v, new_sf, new_sb = [], [], [], []
    n_attn = n_conv = n_rwkv = 0
    for i in range(DEPTH):
        m = modt[i]
        kind = i % 3
        if kind == 0:
            j = n_attn
            qkv = _mod_proj(lay, x, m[0], m[1], attn_w_qkv[j], jnp.zeros((Q_DIM + 2 * KV_DIM,), f32), name="attn_qkv")
            past = cache_attn_k.shape[2]
            o_p = _context_attention(lay, qkv, attn_sink[j])
            o_s = _latent_attention(lay, qkv, cache_attn_k[:, j].reshape(bs, past, KV_DIM),
                                    cache_attn_v[:, j].reshape(bs, past, KV_DIM), attn_sink[j])
            y = jnp.concatenate([o_p, o_s], axis=0)
            w_out, b_out = attn_w_o[j], zero_b
            new_k.append(qkv[:lay.n_prompt, Q_DIM:Q_DIM + KV_DIM].reshape(bp, sp, N_KV_HEADS, HEAD_DIM))
            new_v.append(qkv[:lay.n_prompt, Q_DIM + KV_DIM:].reshape(bp, sp, N_KV_HEADS, HEAD_DIM))
            n_attn += 1
        elif kind == 1:
            j = n_conv
            y = _conformer(lay, x, m, conv_w_pw1[j], conv_b_pw1[j], conv_w_dw[j], conv_b_dw[j], conv_ln_g[j],
                           conv_ln_b[j], conv_w_pw2[j], conv_b_pw2[j])
            w_out, b_out = conv_w_pw2[j], conv_b_pw2[j]
            n_conv += 1
        else:
            j = n_rwkv
            y, sf, sb = _rwkv(lay, x, m, state_rwkv_fwd[:, j], state_rwkv_bwd[:, j], rwkv_mu[j], rwkv_w_rkv[j],
                              rwkv_w0[j], rwkv_w1[j], rwkv_w2[j], rwkv_a0[j], rwkv_a1[j], rwkv_a2[j], rwkv_g1[j],
                              rwkv_g2[j], rwkv_k_k[j], rwkv_k_a[j], rwkv_r_k[j], rwkv_gn_g[j], rwkv_gn_b[j])
            w_out, b_out = rwkv_w_o[j], zero_b
            new_sf.append(sf)
            new_sb.append(sb)
            n_rwkv += 1
        x = _out_proj_ln(lay, y, w_out, b_out, x, m[2], post_ln_g[i, 0], post_ln_b[i, 0], name="mixer_out_ln")
        x = _moe_layer(lay, x, m, moe_w_group[i], moe_b_group[i], moe_w_expert[i], moe_b_expert[i],
                       moe_w_gate_up[i], moe_w_down[i], post_ln_g[i, 1], post_ln_b[i, 1])
    y_prompt = x[:lay.n_prompt].reshape(bp, sp, D)
    y_sample = x[lay.n_prompt:].reshape(bs, ss, D)
    return (y_prompt, y_sample, jnp.stack(new_k, axis=1), jnp.stack(new_v, axis=1),
            jnp.stack(new_sf, axis=1).astype(x_prompt.dtype), jnp.stack(new_sb, axis=1).astype(x_prompt.dtype))
```

```python
import functools

import jax
import jax.numpy as jnp
from jax import lax
from jax.experimental import pallas as pl
from jax.experimental.pallas import tpu as pltpu

f32 = jnp.float32
bf16 = jnp.bfloat16

D = 1024
DEPTH = 4
GRID_W = 64
HEAD_DIM = 64
N_Q_HEADS = D // HEAD_DIM
N_KV_HEADS = N_Q_HEADS // 4
Q_PER_KV = N_Q_HEADS // N_KV_HEADS
Q_DIM = N_Q_HEADS * HEAD_DIM
KV_DIM = N_KV_HEADS * HEAD_DIM
WINDOW = 128
ROPE_BASE = 10000.0
CONV_WIDTH = 31
CONV_PAD = (CONV_WIDTH - 1) // 2
RWKV_HEAD = 64
RWKV_HEADS = D // RWKV_HEAD
GN_EPS = 64e-5
N_GROUPS = 4
EXPERTS_PER_GROUP = 8
N_EXPERTS = N_GROUPS * EXPERTS_PER_GROUP
TOP_K = 2
EXPERT_FF = D // 2
LN_EPS = 1e-5
ALPHA = (2 * DEPTH) ** 0.25

LANES = 128
ROW_TILE = 256
MOE_ROWS = 512
ATT_Q = 128
SCAN_CHUNK = 64
SCAN_ROWS = 256
SCAN_PAIRS = 8
MOD_ROWS = 16
VMEM_LIMIT = 56 * 1024 * 1024
NEG = -1e30
HI = lax.Precision.HIGHEST


def _params(n_axes):
    return pltpu.CompilerParams(dimension_semantics=("arbitrary",) * n_axes,
                                vmem_limit_bytes=VMEM_LIMIT)


def _const_spec(shape):
    nd = len(shape)
    return pl.BlockSpec(shape, lambda *_: (0,) * nd)


class _Layout:
    def __init__(self, bp, sp, bs, ss):
        self.bp, self.sp, self.bs, self.ss = bp, sp, bs, ss
        self.n_prompt = bp * sp
        self.n = bp * sp + bs * ss
        assert sp % ROW_TILE == 0 and ss % ROW_TILE == 0
        self.n_tiles = self.n // ROW_TILE
        self.prompt_tiles = self.n_prompt // ROW_TILE
        self.tiles_per_sseq = ss // ROW_TILE
        self.tiles_per_pseq = sp // ROW_TILE

    def mod_row(self, i):
        return jnp.where(i < self.prompt_tiles, 0, 1 + (i - self.prompt_tiles) // self.tiles_per_sseq)

    def seq_pos(self, i):
        in_p = i < self.prompt_tiles
        pos = jnp.where(in_p, i % self.tiles_per_pseq, (i - self.prompt_tiles) % self.tiles_per_sseq)
        cnt = jnp.where(in_p, self.tiles_per_pseq, self.tiles_per_sseq)
        return pos, cnt

    def row_spec(self, width=D, col=0):
        return pl.BlockSpec((ROW_TILE, width), lambda i: (i, col))

    def mod_spec(self):
        return pl.BlockSpec((None, 1, D), lambda i: (self.mod_row(i), 0, 0))


def _layer_norm(z, g, b, eps=LN_EPS):
    mu = jnp.mean(z, axis=-1, keepdims=True)
    zc = z - mu
    var = jnp.mean(zc * zc, axis=-1, keepdims=True)
    return zc * lax.rsqrt(var + eps) * g + b


def _silu(x):
    return x * jax.nn.sigmoid(x)


def _mod_body(c_ref, w_ref, b_ref, o_ref):
    o_ref[...] = jnp.dot(_silu(c_ref[...]), w_ref[...], precision=HI,
                         preferred_element_type=f32) + b_ref[...]


def _modulation_table(cond, mod_w, mod_b):
    out = pl.pallas_call(
        _mod_body,
        grid=(DEPTH, 6),
        in_specs=[_const_spec((MOD_ROWS, D)),
                  pl.BlockSpec((None, D, D), lambda l, j: (l, 0, j)),
                  pl.BlockSpec((None, 1, D), lambda l, j: (l, 0, j))],
        out_specs=pl.BlockSpec((None, None, MOD_ROWS, D), lambda l, j: (l, j, 0, 0)),
        out_shape=jax.ShapeDtypeStruct((DEPTH, 6, MOD_ROWS, D), f32),
        compiler_params=_params(2),
        name="modulation",
    )(cond, mod_w, mod_b.reshape(DEPTH, 1, 6 * D))
    return out.reshape(DEPTH, 6, MOD_ROWS, 1, D)


def _proj_body(x_ref, sh_ref, sc_ref, w_ref, b_ref, o_ref, *, glu):
    h = x_ref[...] * (1.0 + sc_ref[...]) + sh_ref[...]
    u = jnp.dot(h.astype(bf16), w_ref[...], preferred_element_type=f32) + b_ref[...]
    if glu:
        half = u.shape[1] // 2
        u = u[:, :half] * jax.nn.sigmoid(u[:, half:])
    o_ref[...] = u.astype(o_ref.dtype)


def _mod_proj(lay, x, shift, scale, w, b, *, glu=False, out_dtype=f32, name="proj"):
    k, n = w.shape
    n_out = n // 2 if glu else n
    return pl.pallas_call(
        functools.partial(_proj_body, glu=glu),
        grid=(lay.n_tiles,),
        in_specs=[lay.row_spec(), lay.mod_spec(), lay.mod_spec(),
                  _const_spec((k, n)), _const_spec((1, n))],
        out_specs=lay.row_spec(n_out),
        out_shape=jax.ShapeDtypeStruct((lay.n, n_out), out_dtype),
        compiler_params=_params(1),
        name=name,
    )(x, shift, scale, w.astype(bf16), b.reshape(1, n))


def _out_ln_body(y_ref, w_ref, b_ref, x_ref, gate_ref, g_ref, be_ref, o_ref):
    t = jnp.dot(y_ref[...].astype(bf16), w_ref[...], preferred_element_type=f32) + b_ref[...]
    z = ALPHA * x_ref[...] + gate_ref[...] * t
    o_ref[...] = _layer_norm(z, g_ref[...], be_ref[...])


def _out_proj_ln(lay, y, w, b, x, gate, ln_g, ln_b, name="out_ln"):
    k = w.shape[0]
    return pl.pallas_call(
        _out_ln_body,
        grid=(lay.n_tiles,),
        in_specs=[lay.row_spec(k), _const_spec((k, D)), _const_spec((1, D)),
                  lay.row_spec(), lay.mod_spec(), _const_spec((1, D)), _const_spec((1, D))],
        out_specs=lay.row_spec(),
        out_shape=jax.ShapeDtypeStruct((lay.n, D), f32),
        compiler_params=_params(1),
        name=name,
    )(y, w.astype(bf16), b.reshape(1, D), x, gate, ln_g.reshape(1, D), ln_b.reshape(1, D))


def _sink_column(sink_ref, h, rows_per_head):
    r = lax.broadcasted_iota(jnp.int32, (Q_PER_KV * rows_per_head, 1), 0) // rows_per_head
    col = jnp.full(r.shape, sink_ref[h * Q_PER_KV], f32)
    for g in range(1, Q_PER_KV):
        col = jnp.where(r == g, sink_ref[h * Q_PER_KV + g], col)
    return col


def _stack_heads(q, h):
    return jnp.concatenate(
        [q[:, (h * Q_PER_KV + g) * HEAD_DIM:(h * Q_PER_KV + g + 1) * HEAD_DIM] for g in range(Q_PER_KV)], axis=0)


def _nt(a, b):
    return lax.dot_general(a, b, (((1,), (1,)), ((), ())), preferred_element_type=f32)


def _ctx_attn_body(sink_ref, q_ref, k_ref, v_ref, o_ref):
    rows = q_ref.shape[0]
    q = q_ref[...] * (HEAD_DIM ** -0.5)
    k = k_ref[...]
    v = v_ref[...]
    outs = [None] * N_Q_HEADS
    for h in range(N_KV_HEADS):
        qh = _stack_heads(q, h).astype(bf16)
        kh = k[:, h * HEAD_DIM:(h + 1) * HEAD_DIM].astype(bf16)
        vh = v[:, h * HEAD_DIM:(h + 1) * HEAD_DIM].astype(bf16)
        s = _nt(qh, kh)
        sk = _sink_column(sink_ref, h, rows)
        m = jnp.maximum(jnp.max(s, axis=-1, keepdims=True), sk)
        p = jnp.exp(s - m)
        den = jnp.sum(p, axis=-1, keepdims=True) + jnp.exp(sk - m)
        o = jnp.dot(p.astype(bf16), vh, preferred_element_type=f32) / den
        for g in range(Q_PER_KV):
            outs[h * Q_PER_KV + g] = o[g * rows:(g + 1) * rows]
    o_ref[...] = jnp.concatenate(outs, axis=1).astype(o_ref.dtype)


def _context_attention(lay, qkv, sink):
    sp = lay.sp
    kcol = Q_DIM // KV_DIM
    return pl.pallas_call(
        _ctx_attn_body,
        grid=(lay.bp,),
        in_specs=[pl.BlockSpec(memory_space=pltpu.SMEM),
                  pl.BlockSpec((sp, Q_DIM), lambda b: (b, 0)),
                  pl.BlockSpec((sp, KV_DIM), lambda b: (b, kcol)),
                  pl.BlockSpec((sp, KV_DIM), lambda b: (b, kcol + 1))],
        out_specs=pl.BlockSpec((sp, Q_DIM), lambda b: (b, 0)),
        out_shape=jax.ShapeDtypeStruct((lay.n_prompt, Q_DIM), bf16),
        compiler_params=_params(1),
        name="ctx_attn",
    )(sink, qkv, qkv, qkv)


def _rope_tables(t_len):
    t = jnp.arange(t_len)
    row = (t // GRID_W).astype(f32)
    col = (t % GRID_W).astype(f32)
    quarter = HEAD_DIM // 4
    d = jnp.arange(LANES) % HEAD_DIM
    inv = ROPE_BASE ** (-(d % quarter).astype(f32) / quarter)
    pos = jnp.where((d < HEAD_DIM // 2)[None, :], row[:, None], col[:, None])
    ang = pos * inv[None, :]
    cs, sn = jnp.cos(ang), jnp.sin(ang)
    first = ((d % (2 * quarter)) < quarter)[None, :]
    return cs, jnp.where(first, -sn, 0.0), jnp.where(first, 0.0, sn)


def _rope(x, cs, sa, sb):
    quarter = HEAD_DIM // 4
    outs = []
    for c in range(x.shape[1] // LANES):
        xc = x[:, c * LANES:(c + 1) * LANES]
        outs.append(xc * cs + pltpu.roll(xc, LANES - quarter, axis=1) * sa + pltpu.roll(xc, quarter, axis=1) * sb)
    return outs[0] if len(outs) == 1 else jnp.concatenate(outs, axis=1)


def _lat_attn_body(sink_ref, q_ref, k_ref, v_ref, kc_ref, vc_ref, cs_ref, sa_ref, sb_ref, o_ref):
    j = pl.program_id(1)
    t_len = k_ref.shape[0]
    win = 3 * ATT_Q
    start = pl.multiple_of(jnp.clip((j - 1) * ATT_Q, 0, t_len - win), ATT_Q)
    q0 = pl.multiple_of(j * ATT_Q, ATT_Q)
    q = _rope(q_ref[...], cs_ref[pl.ds(q0, ATT_Q), :], sa_ref[pl.ds(q0, ATT_Q), :],
              sb_ref[pl.ds(q0, ATT_Q), :]) * (HEAD_DIM ** -0.5)
    kw = _rope(k_ref[pl.ds(start, win), :], cs_ref[pl.ds(start, win), :], sa_ref[pl.ds(start, win), :],
               sb_ref[pl.ds(start, win), :])
    vw = v_ref[pl.ds(start, win), :]
    kc = kc_ref[...]
    vc = vc_ref[...]
    rows = Q_PER_KV * ATT_Q
    qpos = q0 + lax.broadcasted_iota(jnp.int32, (rows, win), 0) % ATT_Q
    kpos = start + lax.broadcasted_iota(jnp.int32, (rows, win), 1)
    valid = jnp.abs(qpos - kpos) <= WINDOW
    outs = [None] * N_Q_HEADS
    for h in range(N_KV_HEADS):
        sl = slice(h * HEAD_DIM, (h + 1) * HEAD_DIM)
        qh = _stack_heads(q, h).astype(bf16)
        s_c = _nt(qh, kc[:, sl].astype(bf16))
        s_l = jnp.where(valid, _nt(qh, kw[:, sl].astype(bf16)), NEG)
        sk = _sink_column(sink_ref, h, ATT_Q)
        m = jnp.maximum(jnp.maximum(jnp.max(s_c, axis=-1, keepdims=True),
                                    jnp.max(s_l, axis=-1, keepdims=True)), sk)
        p_c = jnp.exp(s_c - m)
        p_l = jnp.exp(s_l - m)
        den = jnp.sum(p_c, axis=-1, keepdims=True) + jnp.sum(p_l, axis=-1, keepdims=True) + jnp.exp(sk - m)
        o = (jnp.dot(p_c.astype(bf16), vc[:, sl].astype(bf16), preferred_element_type=f32)
             + jnp.dot(p_l.astype(bf16), vw[:, sl].astype(bf16), preferred_element_type=f32)) / den
        for g in range(Q_PER_KV):
            outs[h * Q_PER_KV + g] = o[g * ATT_Q:(g + 1) * ATT_Q]
    o_ref[...] = jnp.concatenate(outs, axis=1).astype(o_ref.dtype)


def _latent_attention(lay, qkv, kc, vc, sink):
    ss = lay.ss
    past = kc.shape[1]
    nq = ss // ATT_Q
    q_base = lay.n_prompt // ATT_Q
    s_base = lay.n_prompt // ss
    assert lay.n_prompt % ss == 0
    kcol = Q_DIM // KV_DIM
    cs, sa, sb = _rope_tables(ss)
    return pl.pallas_call(
        _lat_attn_body,
        grid=(lay.bs, nq),
        in_specs=[pl.BlockSpec(memory_space=pltpu.SMEM),
                  pl.BlockSpec((ATT_Q, Q_DIM), lambda b, j: (q_base + b * nq + j, 0)),
                  pl.BlockSpec((ss, KV_DIM), lambda b, j: (s_base + b, kcol)),
                  pl.BlockSpec((ss, KV_DIM), lambda b, j: (s_base + b, kcol + 1)),
                  pl.BlockSpec((None, past, KV_DIM), lambda b, j: (b, 0, 0)),
                  pl.BlockSpec((None, past, KV_DIM), lambda b, j: (b, 0, 0)),
                  _const_spec((ss, LANES)), _const_spec((ss, LANES)), _const_spec((ss, LANES))],
        out_specs=pl.BlockSpec((ATT_Q, Q_DIM), lambda b, j: (b * nq + j, 0)),
        out_shape=jax.ShapeDtypeStruct((lay.bs * ss, Q_DIM), bf16),
        compiler_params=_params(2),
        name="lat_attn",
    )(sink, qkv, qkv, qkv, kc, vc, cs, sa, sb)


CONV_HALO = 16
CONV_SUB = 32


def _dwconv_body(lay, u_ref, prev_ref, next_ref, w_ref, b_ref, g_ref, be_ref, o_ref, buf_ref):
    i = pl.program_id(0)
    pos, cnt = lay.seq_pos(i)
    buf_ref[pl.ds(0, CONV_HALO), :] = jnp.where(pos > 0, prev_ref[...], 0.0)
    buf_ref[pl.ds(CONV_HALO, ROW_TILE), :] = u_ref[...]
    buf_ref[pl.ds(CONV_HALO + ROW_TILE, CONV_HALO), :] = jnp.where(pos < cnt - 1, next_ref[...], 0.0)
    w = w_ref[...]
    off = CONV_HALO - CONV_PAD

    for s in range(ROW_TILE // CONV_SUB):
        r0 = s * CONV_SUB
        acc = jnp.zeros((CONV_SUB, D), f32)
        for t in range(CONV_WIDTH):
            acc = acc + buf_ref[pl.ds(r0 + off + t, CONV_SUB), :] * w[t:t + 1, :]
        z = _layer_norm(acc + b_ref[...], g_ref[...], be_ref[...])
        o_ref[pl.ds(r0, CONV_SUB), :] = _silu(z).astype(o_ref.dtype)


def _dwconv_ln_swish(lay, u, w_dw, b_dw, ln_g, ln_b):
    hpt = ROW_TILE // CONV_HALO
    last = lay.n // CONV_HALO - 1
    return pl.pallas_call(
        functools.partial(_dwconv_body, lay),
        grid=(lay.n_tiles,),
        in_specs=[lay.row_spec(),
                  pl.BlockSpec((CONV_HALO, D), lambda i: (jnp.maximum(i * hpt - 1, 0), 0)),
                  pl.BlockSpec((CONV_HALO, D), lambda i: (jnp.minimum((i + 1) * hpt, last), 0)),
                  _const_spec((CONV_WIDTH, D)), _const_spec((1, D)), _const_spec((1, D)), _const_spec((1, D))],
        out_specs=lay.row_spec(),
        out_shape=jax.ShapeDtypeStruct((lay.n, D), bf16),
        scratch_shapes=[pltpu.VMEM((ROW_TILE + 2 * CONV_HALO, D), f32)],
        compiler_params=_params(1),
        name="dwconv",
    )(u, u, u, w_dw, b_dw.reshape(1, D), ln_g.reshape(1, D), ln_b.reshape(1, D))


def _conformer(lay, x, m, w_pw1, b_pw1, w_dw, b_dw, ln_g, ln_b, w_pw2, b_pw2):
    u = _mod_proj(lay, x, m[0], m[1], w_pw1, b_pw1, glu=True, name="conv_pw1_glu")
    return _dwconv_ln_swish(lay, u, w_dw, b_dw, ln_g, ln_b)


LORA_PAD = 128
GATE_PAD = 256


def _bdot(a, w):
    return jnp.dot(a.astype(bf16), w, preferred_element_type=f32)


def _rwkv_proj_body(lay, x_ref, prev_ref, next_ref, sh_ref, sc_ref, mu_ref, wrkv_ref, w0_ref, w1_ref, w2_ref,
                    a0_ref, a1_ref, a2_ref, g1_ref, g2_ref, r_ref, k_ref, v_ref, g_ref, lw_ref, a_ref):
    i = pl.program_id(0)
    pos, cnt = lay.seq_pos(i)
    sc = 1.0 + sc_ref[...]
    sh = sh_ref[...]
    h = x_ref[...] * sc + sh
    hb = prev_ref.shape[0]
    h_prev = jnp.where(pos > 0, prev_ref[hb - 1:hb, :] * sc + sh, 0.0)
    h_next = jnp.where(pos < cnt - 1, next_ref[0:1, :] * sc + sh, 0.0)
    row = lax.broadcasted_iota(jnp.int32, (ROW_TILE, 1), 0)
    below = jnp.where(row == 0, h_prev, pltpu.roll(h, 1, axis=0))
    above = jnp.where(row == ROW_TILE - 1, h_next, pltpu.roll(h, ROW_TILE - 1, axis=0))
    xx = 0.5 * (below + above) - h

    def mix(j):
        return (h + xx * mu_ref[j:j + 1, :]).astype(bf16)

    r_ref[...] = _bdot(mix(0), wrkv_ref[0])
    k_ref[...] = _bdot(mix(2), wrkv_ref[1])
    v_ref[...] = _bdot(mix(3), wrkv_ref[2])
    g_ref[...] = _bdot(jax.nn.sigmoid(_bdot(mix(5), g1_ref[...])), g2_ref[...])
    xw = mix(1)
    xa = mix(4)
    for d in range(2):
        z = w0_ref[d:d + 1, :] + _bdot(jnp.tanh(_bdot(xw, w1_ref[d])), w2_ref[d])
        softplus = jnp.maximum(-z, 0.0) + jnp.log(1.0 + jnp.exp(-jnp.abs(z)))
        lw_ref[d] = -jnp.exp(-softplus - 0.5)
        a_ref[d] = jax.nn.sigmoid(a0_ref[d:d + 1, :] + _bdot(_bdot(xa, a1_ref[d]), a2_ref[d]))


def _pad_to(w, axis, size):
    pad = [(0, 0)] * w.ndim
    pad[axis] = (0, size - w.shape[axis])
    return jnp.pad(w, pad)


def _rwkv_project(lay, x, m, mu, w_rkv, w0, w1, w2, a0, a1, a2, g1, g2):
    halo = 8
    hpt = ROW_TILE // halo
    last = lay.n // halo - 1
    nd = (lay.n, D)
    consts = [mu, w_rkv.astype(bf16), w0,
              _pad_to(w1, 2, LORA_PAD).astype(bf16), _pad_to(w2, 1, LORA_PAD).astype(bf16), a0,
              _pad_to(a1, 2, LORA_PAD).astype(bf16), _pad_to(a2, 1, LORA_PAD).astype(bf16),
              _pad_to(g1, 1, GATE_PAD).astype(bf16), _pad_to(g2, 0, GATE_PAD).astype(bf16)]
    return pl.pallas_call(
        functools.partial(_rwkv_proj_body, lay),
        grid=(lay.n_tiles,),
        in_specs=[lay.row_spec(),
                  pl.BlockSpec((halo, D), lambda i: (jnp.maximum(i * hpt - 1, 0), 0)),
                  pl.BlockSpec((halo, D), lambda i: (jnp.minimum((i + 1) * hpt, last), 0)),
                  lay.mod_spec(), lay.mod_spec()] + [_const_spec(c.shape) for c in consts],
        out_specs=[lay.row_spec()] * 4 + [pl.BlockSpec((2, ROW_TILE, D), lambda i: (0, i, 0))] * 2,
        out_shape=[jax.ShapeDtypeStruct(nd, f32)] * 4 + [jax.ShapeDtypeStruct((2,) + nd, f32)] * 2,
        compiler_params=_params(1),
        name="rwkv_proj",
    )(x, x, x, m[0], m[1], *consts)


def _split_dot(x, c, parts):
    acc = None
    for _ in range(parts):
        hi = x.astype(bf16)
        t = jnp.dot(hi, c, preferred_element_type=f32)
        acc = t if acc is None else acc + t
        x = x - hi.astype(f32)
    return acc


def _split_dot_left(c, x, parts):
    acc = None
    for _ in range(parts):
        hi = x.astype(bf16)
        t = jnp.dot(c, hi, preferred_element_type=f32)
        acc = t if acc is None else acc + t
        x = x - hi.astype(f32)
    return acc


def _mm(a, b):
    return jnp.dot(a.astype(bf16), b.astype(bf16), preferred_element_type=f32)


def _split3(x):
    x1 = x.astype(bf16)
    r1 = x - x1.astype(f32)
    x2 = r1.astype(bf16)
    x3 = (r1 - x2.astype(f32)).astype(bf16)
    return x1, x2, x3


def _mm_f32(a3, b):
    b1, b2, b3 = _split3(b)
    n = a3[0].shape[0]
    r1 = jnp.dot(jnp.concatenate(a3, axis=0), b1, preferred_element_type=f32)
    r2 = jnp.dot(jnp.concatenate(a3[:2], axis=0), b2, preferred_element_type=f32)
    r3 = jnp.dot(a3[0], b3, preferred_element_type=f32)
    small = r1[2 * n:] + r2[n:] + r3
    return r1[:n] + ((r1[n:2 * n] + r2[:n]) + small)


def _mm_nt(a, b):
    return lax.dot_general(a.astype(bf16), b.astype(bf16), (((1,), (1,)), ((), ())), preferred_element_type=f32)


def _scan_body(lay, rev, r_ref, k_ref, v_ref, lw_ref, a_ref, kk_ref, ka_ref, rk_ref, s0_ref, tri_ref, bd_ref,
               y_ref, bon_ref, sfin_ref, state_ref):
    t_idx = pl.program_id(1)
    i = (lay.n_tiles - 1 - t_idx) if rev else t_idx
    pos, cnt = lay.seq_pos(i)
    first = (pos == cnt - 1) if rev else (pos == 0)
    final = (pos == 0) if rev else (pos == cnt - 1)
    c_len = SCAN_CHUNK
    n_chunks = SCAN_ROWS // c_len

    @pl.when(first)
    def _():
        state_ref[...] = s0_ref[...]

    lane = lax.broadcasted_iota(jnp.int32, (1, LANES), 1)
    head0 = lane < RWKV_HEAD
    rr = lax.broadcasted_iota(jnp.int32, (2 * c_len, 2 * c_len), 0)
    cc = lax.broadcasted_iota(jnp.int32, (2 * c_len, 2 * c_len), 1)
    same = (rr // c_len) == (cc // c_len)
    tt = rr % c_len
    ss = cc % c_len
    strict = same & ((tt < ss) if rev else (tt > ss))
    incl = same & ((tt <= ss) if rev else (tt >= ss))
    tri = tri_ref[...]
    bd = bd_ref[...]

    def stack(xc):
        return jnp.concatenate([jnp.where(head0, xc, 0.0), jnp.where(head0, 0.0, xc)], axis=0)

    def pair_chunk(rows, p):
        cols = slice(p * LANES, (p + 1) * LANES)
        r = r_ref[rows, cols]
        k = k_ref[rows, cols]
        v = v_ref[rows, cols]
        lw = lw_ref[rows, cols]
        a = a_ref[rows, cols]
        kk = k * kk_ref[:, cols]
        nrm = jnp.sqrt(_split_dot(kk * kk, bd, 3))
        kk = kk / jnp.maximum(nrm, 1e-12)
        b = kk * a
        kd = k * (1.0 + (a - 1.0) * ka_ref[:, cols])
        bon_ref[rows, cols] = _split_dot(r * kd * rk_ref[:, cols], bd, 3) * v
        cum = _split_dot_left(tri, lw, 3)
        tot = cum[0:1, :] if rev else cum[c_len - 1:c_len, :]
        e_tot = jnp.exp(tot)
        e_neg = jnp.exp(-cum)
        a_t = -kk * jnp.exp(cum - lw)
        r_t = r * jnp.exp(cum)
        b_t = b * e_neg
        k_t = kd * e_neg
        ar = jnp.concatenate([stack(a_t), stack(r_t)], axis=0)
        bk = jnp.concatenate([stack(b_t), stack(k_t)], axis=0)
        m = _mm_nt(ar, bk)
        h2 = 2 * c_len
        n_ab = jnp.where(strict, m[:h2, :h2], 0.0)
        a_ak = jnp.where(strict, m[:h2, h2:], 0.0)
        a_rb = jnp.where(incl, m[h2:, :h2], 0.0)
        a_rk = jnp.where(incl, m[h2:, h2:], 0.0)
        state = state_ref[p]
        x0 = _mm_nt(ar, state)
        v_st = stack(v)
        sa = x0[:h2] + _mm(a_ak, v_st)
        pw = n_ab
        span = 1
        while span < c_len:
            pw3 = _split3(pw)
            sa = sa + _mm_f32(pw3, sa)
            span *= 2
            if span < c_len:
                pw = _mm_f32(pw3, pw)
        sav = jnp.concatenate([sa, v_st], axis=0)
        y_st = x0[h2:] + _mm(jnp.concatenate([a_rb, a_rk], axis=1), sav)
        y_ref[rows, cols] = y_st[:c_len] + y_st[c_len:]
        state_ref[p] = state * e_tot + _mm(sav.T, bk * e_tot)

    def chunk(c, carry):
        ce = (n_chunks - 1 - c) if rev else c
        rows = pl.ds(pl.multiple_of(ce * c_len, c_len), c_len)
        for p in range(SCAN_PAIRS):
            pair_chunk(rows, p)
        return carry

    lax.fori_loop(0, n_chunks, chunk, 0)

    @pl.when(final)
    def _():
        sfin_ref[...] = state_ref[...]


def _block_diag_states(s):
    b = s.shape[0]
    s = s.reshape(b, RWKV_HEADS // 2, 2, RWKV_HEAD, RWKV_HEAD)
    eye = jnp.eye(2, dtype=s.dtype)
    out = s[:, :, :, :, None, :] * eye[None, None, :, None, :, None]
    return out.reshape(b, RWKV_HEADS // 2, LANES, LANES)


def _diag_blocks(w):
    b = w.shape[0]
    w = w.reshape(b, RWKV_HEADS // 2, 2, RWKV_HEAD, 2, RWKV_HEAD)
    return jnp.stack([w[:, :, 0, :, 0, :], w[:, :, 1, :, 1, :]], axis=2).reshape(b, RWKV_HEADS, RWKV_HEAD, RWKV_HEAD)


def _rwkv_scan(lay, rev, r, k, v, lw, a, k_k, k_a, r_k, s0):
    d = 1 if rev else 0
    n_seq = lay.bp + lay.bs
    c_len = SCAN_CHUNK
    idx = jnp.arange(c_len)
    tri = ((idx[:, None] <= idx[None, :]) if rev else (idx[:, None] >= idx[None, :])).astype(bf16)
    hh = jnp.arange(LANES) // RWKV_HEAD
    bd = (hh[:, None] == hh[None, :]).astype(bf16)

    def tile(t):
        return (lay.n_tiles - 1 - t) if rev else t

    def seq(t):
        i = tile(t)
        return jnp.where(i < lay.prompt_tiles, i // lay.tiles_per_pseq,
                         lay.bp + (i - lay.prompt_tiles) // lay.tiles_per_sseq)

    width = SCAN_PAIRS * LANES
    row = pl.BlockSpec((SCAN_ROWS, width), lambda p, t: (tile(t), p))
    row_d = pl.BlockSpec((None, SCAN_ROWS, width), lambda p, t: (d, tile(t), p))
    par = pl.BlockSpec((1, width), lambda p, t: (0, p))
    st = pl.BlockSpec((None, SCAN_PAIRS, LANES, LANES), lambda p, t: (seq(t), p, 0, 0))
    return pl.pallas_call(
        functools.partial(_scan_body, lay, rev),
        grid=(RWKV_HEADS // 2 // SCAN_PAIRS, lay.n_tiles),
        in_specs=[row, row, row, row_d, row_d, par, par, par, st,
                  _const_spec((c_len, c_len)), _const_spec((LANES, LANES))],
        out_specs=[row, row, st],
        out_shape=[jax.ShapeDtypeStruct((lay.n, D), f32), jax.ShapeDtypeStruct((lay.n, D), f32),
                   jax.ShapeDtypeStruct((n_seq, RWKV_HEADS // 2, LANES, LANES), f32)],
        scratch_shapes=[pltpu.VMEM((SCAN_PAIRS, LANES, LANES), f32)],
        compiler_params=_params(2),
        name="rwkv_scan_bwd" if rev else "rwkv_scan_fwd",
    )(r, k, v, lw, a, k_k.reshape(1, D), k_a.reshape(1, D), r_k.reshape(1, D), s0, tri, bd)


def _rwkv_out_body(yf_ref, yb_ref, bf_ref, bb_ref, g_ref, gg_ref, gb_ref, bd_ref, o_ref):
    y = yf_ref[...] + yb_ref[...]
    bd = bd_ref[...]
    inv = 1.0 / RWKV_HEAD
    outs = []
    for c in range(D // LANES):
        yc = y[:, c * LANES:(c + 1) * LANES]
        cen = yc - _split_dot(yc, bd, 3) * inv
        var = _split_dot(cen * cen, bd, 3) * inv
        outs.append(cen * lax.rsqrt(var + GN_EPS))
    yn = jnp.concatenate(outs, axis=1) * gg_ref[...] + gb_ref[...]
    o_ref[...] = ((yn + bf_ref[...] + bb_ref[...]) * g_ref[...]).astype(o_ref.dtype)


def _rwkv(lay, x, m, s0_f, s0_b, mu, w_rkv, w0, w1, w2, a0, a1, a2, g1, g2, k_k, k_a, r_k, gn_g, gn_b):
    r, k, v, g, lw, a = _rwkv_project(lay, x, m, mu, w_rkv, w0, w1, w2, a0, a1, a2, g1, g2)
    zeros = jnp.zeros((lay.bp, RWKV_HEADS // 2, LANES, LANES), f32)
    outs = []
    for rev, s0 in ((False, s0_f), (True, s0_b)):
        s_all = jnp.concatenate([zeros, _block_diag_states(s0.astype(f32))], axis=0)
        outs.append(_rwkv_scan(lay, rev, r, k, v, lw, a, k_k, k_a, r_k, s_all))
    hh = jnp.arange(LANES) // RWKV_HEAD
    bd = (hh[:, None] == hh[None, :]).astype(bf16)
    y = pl.pallas_call(
        _rwkv_out_body,
        grid=(lay.n_tiles,),
        in_specs=[lay.row_spec()] * 5 + [_const_spec((1, D)), _const_spec((1, D)), _const_spec((LANES, LANES))],
        out_specs=lay.row_spec(),
        out_shape=jax.ShapeDtypeStruct((lay.n, D), bf16),
        compiler_params=_params(1),
        name="rwkv_groupnorm_gate",
    )(outs[0][0], outs[1][0], outs[0][1], outs[1][1], g, gn_g.reshape(1, D), gn_b.reshape(1, D), bd)
    return y, _diag_blocks(outs[0][2][:lay.bp]), _diag_blocks(outs[1][2][:lay.bp])


def _router_body(x_ref, sh_ref, sc_ref, w_ref, b_ref, h_ref, route_ref):
    h = x_ref[...] * (1.0 + sc_ref[...]) + sh_ref[...]
    h_ref[...] = h.astype(h_ref.dtype)
    lg = jnp.dot(h, w_ref[...], precision=HI, preferred_element_type=f32) + b_ref[...]
    lane = lax.broadcasted_iota(jnp.int32, lg.shape, 1)

    def top1(vals):
        mx = jnp.max(vals, axis=-1, keepdims=True)
        idx = jnp.min(jnp.where(vals == mx, lane, LANES), axis=-1, keepdims=True)
        return mx, idx

    gl = jnp.where(lane < N_GROUPS, lg, NEG)
    g_max, g_idx = top1(gl)
    p_g = 1.0 / jnp.sum(jnp.exp(gl - g_max), axis=-1, keepdims=True)
    lo = N_GROUPS + g_idx * EXPERTS_PER_GROUP
    el = jnp.where((lane >= lo) & (lane < lo + EXPERTS_PER_GROUP), lg, NEG)
    m1, i1 = top1(el)
    m2, i2 = top1(jnp.where(lane == i1, NEG, el))
    e2 = jnp.exp(m2 - m1)
    w1 = p_g / (1.0 + e2)
    w2 = p_g * e2 / (1.0 + e2)
    route = jnp.where(lane == 0, (i1 - N_GROUPS).astype(f32),
                      jnp.where(lane == 1, (i2 - N_GROUPS).astype(f32),
                                jnp.where(lane == 2, w1, jnp.where(lane == 3, w2, 0.0))))
    route_ref[...] = route


def _expert_body(be_ref, nu_ref, x_ref, wgu_ref, wd_ref, o_ref, wgu_s, wd_s):
    i = pl.program_id(0)
    used = i < nu_ref[0]

    @pl.when(used & ((i == 0) | (be_ref[i] != be_ref[jnp.maximum(i - 1, 0)])))
    def _():
        wgu_s[...] = wgu_ref[...].astype(bf16)
        wd_s[...] = wd_ref[...].astype(bf16)

    @pl.when(used)
    def _():
        u = jnp.dot(x_ref[...], wgu_s[...], preferred_element_type=f32)
        act = _silu(u[:, :EXPERT_FF]) * u[:, EXPERT_FF:]
        o_ref[...] = jnp.dot(act.astype(bf16), wd_s[...], preferred_element_type=f32)

    @pl.when(jnp.logical_not(used))
    def _():
        o_ref[...] = jnp.zeros_like(o_ref)


def _moe_out_body(y_ref0, y_ref1, route_ref, x_ref, gate_ref, g_ref, be_ref, o_ref):
    route = route_ref[...]
    y = route[:, 2:3] * y_ref0[...] + route[:, 3:4] * y_ref1[...]
    z = ALPHA * x_ref[...] + gate_ref[...] * y
    o_ref[...] = _layer_norm(z, g_ref[...], be_ref[...])


def _moe_layer(lay, x, m, w_group, b_group, w_expert, b_expert, w_gu, w_down, ln_g, ln_b, layer=None):
    if layer is None:
        w_gu, w_down, layer = w_gu[None], w_down[None], 0
    n = lay.n
    n_route = N_GROUPS + N_EXPERTS
    w_r = _pad_to(jnp.concatenate([w_group, w_expert], axis=1).astype(f32), 1, LANES)
    b_r = _pad_to(jnp.concatenate([b_group, b_expert]).astype(f32).reshape(1, n_route), 1, LANES)
    h, route = pl.pallas_call(
        _router_body,
        grid=(lay.n_tiles,),
        in_specs=[lay.row_spec(), lay.mod_spec(), lay.mod_spec(), _const_spec((D, LANES)), _const_spec((1, LANES))],
        out_specs=[lay.row_spec(), lay.row_spec(LANES)],
        out_shape=[jax.ShapeDtypeStruct((n, D), bf16), jax.ShapeDtypeStruct((n, LANES), f32)],
        compiler_params=_params(1),
        name="moe_router",
    )(x, m[3], m[4], w_r, b_r)

    s = n * TOP_K
    e_flat = route[:, :TOP_K].astype(jnp.int32).reshape(-1)
    pair = jnp.arange(s, dtype=jnp.int32)
    e_sorted, order = lax.sort((e_flat, pair), num_keys=1, is_stable=True)
    experts = jnp.arange(N_EXPERTS, dtype=jnp.int32)
    counts = jnp.sum((e_flat[:, None] == experts[None, :]).astype(jnp.int32), axis=0)
    padded = (counts + MOE_ROWS - 1) // MOE_ROWS * MOE_ROWS
    pad_end = jnp.cumsum(padded)
    pad_start = pad_end - padded
    start = jnp.cumsum(counts) - counts
    dest_sorted = pad_start[e_sorted] + pair - start[e_sorted]
    n_blocks = -(-s // MOE_ROWS) + N_EXPERTS
    p_rows = n_blocks * MOE_ROWS
    block_start = jnp.arange(n_blocks, dtype=jnp.int32) * MOE_ROWS
    block_expert = jnp.minimum(jnp.sum((pad_end[None, :] <= block_start[:, None]).astype(jnp.int32), axis=1),
                               N_EXPERTS - 1)
    n_used = (pad_end[N_EXPERTS - 1] // MOE_ROWS).astype(jnp.int32).reshape(1)
    slot = jnp.arange(p_rows, dtype=jnp.int32)
    slot_expert = jnp.repeat(block_expert, MOE_ROWS)
    in_run = jnp.minimum(slot - pad_start[slot_expert], jnp.maximum(counts[slot_expert] - 1, 0))
    slot_token = order[jnp.clip(start[slot_expert] + in_run, 0, s - 1)] // TOP_K
    _, dest = lax.sort((order, dest_sorted), num_keys=1)
    dest = dest.reshape(n, TOP_K)

    xb = h.at[slot_token].get(mode="promise_in_bounds")
    yb = pl.pallas_call(
        _expert_body,
        grid_spec=pltpu.PrefetchScalarGridSpec(
            num_scalar_prefetch=2,
            grid=(n_blocks,),
            in_specs=[pl.BlockSpec((MOE_ROWS, D), lambda i, be, nu: (jnp.minimum(i, nu[0] - 1), 0)),
                      pl.BlockSpec((None, None, D, 2 * EXPERT_FF), lambda i, be, nu: (layer, be[i], 0, 0)),
                      pl.BlockSpec((None, None, EXPERT_FF, D), lambda i, be, nu: (layer, be[i], 0, 0))],
            out_specs=pl.BlockSpec((MOE_ROWS, D), lambda i, be, nu: (i, 0)),
            scratch_shapes=[pltpu.VMEM((D, 2 * EXPERT_FF), bf16), pltpu.VMEM((EXPERT_FF, D), bf16)]),
        out_shape=jax.ShapeDtypeStruct((p_rows, D), f32),
        compiler_params=_params(1),
        name="moe_experts",
    )(block_expert, n_used, xb, w_gu, w_down)
    y0 = yb.at[dest[:, 0]].get(mode="promise_in_bounds")
    y1 = yb.at[dest[:, 1]].get(mode="promise_in_bounds")

    return pl.pallas_call(
        _moe_out_body,
        grid=(lay.n_tiles,),
        in_specs=[lay.row_spec(), lay.row_spec(), lay.row_spec(LANES), lay.row_spec(), lay.mod_spec(),
                  _const_spec((1, D)), _const_spec((1, D))],
        out_specs=lay.row_spec(),
        out_shape=jax.ShapeDtypeStruct((n, D), f32),
        compiler_params=_params(1),
        name="moe_combine_ln",
    )(y0, y1, route, x, m[5], ln_g.reshape(1, D), ln_b.reshape(1, D))


def kernel(x_prompt, x_sample, c, c_ctx, cache_attn_k, cache_attn_v, state_rwkv_fwd, state_rwkv_bwd, mod_w, mod_b, post_ln_g, post_ln_b, attn_w_qkv, attn_w_o, attn_sink, conv_w_pw1, conv_b_pw1, conv_w_dw, conv_b_dw, conv_ln_g, conv_ln_b, conv_w_pw2, conv_b_pw2, rwkv_mu, rwkv_w_rkv, rwkv_w0, rwkv_w1, rwkv_w2, rwkv_a0, rwkv_a1, rwkv_a2, rwkv_g1, rwkv_g2, rwkv_k_k, rwkv_k_a, rwkv_r_k, rwkv_gn_g, rwkv_gn_b, rwkv_w_o, moe_w_group, moe_b_group, moe_w_expert, moe_b_expert, moe_w_gate_up, moe_w_down):
    bp, sp, _ = x_prompt.shape
    bs, ss, _ = x_sample.shape
    assert 1 + bs <= MOD_ROWS
    lay = _Layout(bp, sp, bs, ss)
    x = jnp.concatenate([x_prompt.reshape(bp * sp, D), x_sample.reshape(bs * ss, D)], axis=0)
    cond = jnp.concatenate([c_ctx[None, :], c, jnp.zeros((MOD_ROWS - 1 - bs, D), f32)], axis=0)
    modt = _modulation_table(cond, mod_w, mod_b)
    zero_b = jnp.zeros((D,), f32)
    new_k, new_v, new_sf, new_sb = [], [], [], []
    n_attn = n_conv = n_rwkv = 0
    for i in range(DEPTH):
        m = modt[i]
        kind = i % 3
        if kind == 0:
            j = n_attn
            qkv = _mod_proj(lay, x, m[0], m[1], attn_w_qkv[j], jnp.zeros((Q_DIM + 2 * KV_DIM,), f32), name="attn_qkv")
            past = cache_attn_k.shape[2]
            o_p = _context_attention(lay, qkv, attn_sink[j])
            o_s = _latent_attention(lay, qkv, cache_attn_k[:, j].reshape(bs, past, KV_DIM),
                                    cache_attn_v[:, j].reshape(bs, past, KV_DIM), attn_sink[j])
            y = jnp.concatenate([o_p, o_s], axis=0)
            w_out, b_out = attn_w_o[j], zero_b
            new_k.append(qkv[:lay.n_prompt, Q_DIM:Q_DIM + KV_DIM].reshape(bp, sp, N_KV_HEADS, HEAD_DIM))
            new_v.append(qkv[:lay.n_prompt, Q_DIM + KV_DIM:].reshape(bp, sp, N_KV_HEADS, HEAD_DIM))
            n_attn += 1
        elif kind == 1:
            j = n_conv
            y = _conformer(lay, x, m, conv_w_pw1[j], conv_b_pw1[j], conv_w_dw[j], conv_b_dw[j], conv_ln_g[j],
                           conv_ln_b[j], conv_w_pw2[j], conv_b_pw2[j])
            w_out, b_out = conv_w_pw2[j], conv_b_pw2[j]
            n_conv += 1
        else:
            j = n_rwkv
            y, sf, sb = _rwkv(lay, x, m, state_rwkv_fwd[:, j], state_rwkv_bwd[:, j], rwkv_mu[j], rwkv_w_rkv[j],
                              rwkv_w0[j], rwkv_w1[j], rwkv_w2[j], rwkv_a0[j], rwkv_a1[j], rwkv_a2[j], rwkv_g1[j],
                              rwkv_g2[j], rwkv_k_k[j], rwkv_k_a[j], rwkv_r_k[j], rwkv_gn_g[j], rwkv_gn_b[j])
            w_out, b_out = rwkv_w_o[j], zero_b
            new_sf.append(sf)
            new_sb.append(sb)
            n_rwkv += 1
        x = _out_proj_ln(lay, y, w_out, b_out, x, m[2], post_ln_g[i, 0], post_ln_b[i, 0], name="mixer_out_ln")
        x = _moe_layer(lay, x, m, moe_w_group[i], moe_b_group[i], moe_w_expert[i], moe_b_expert[i],
                       moe_w_gate_up, moe_w_down, post_ln_g[i, 1], post_ln_b[i, 1], layer=i)
    y_prompt = x[:lay.n_prompt].reshape(bp, sp, D)
    y_sample = x[lay.n_prompt:].reshape(bs, ss, D)
    return (y_prompt, y_sample, jnp.stack(new_k, axis=1), jnp.stack(new_v, axis=1),
            jnp.stack(new_sf, axis=1).astype(x_prompt.dtype), jnp.stack(new_sb, axis=1).astype(x_prompt.dtype))
```

```python
import functools

import jax
import jax.numpy as jnp
from jax import lax
from jax.experimental import pallas as pl
from jax.experimental.pallas import tpu as pltpu

f32 = jnp.float32
bf16 = jnp.bfloat16

D = 1024
DEPTH = 4
GRID_W = 64
HEAD_DIM = 64
N_Q_HEADS = D // HEAD_DIM
N_KV_HEADS = N_Q_HEADS // 4
Q_PER_KV = N_Q_HEADS // N_KV_HEADS
Q_DIM = N_Q_HEADS * HEAD_DIM
KV_DIM = N_KV_HEADS * HEAD_DIM
WINDOW = 128
ROPE_BASE = 10000.0
CONV_WIDTH = 31
CONV_PAD = (CONV_WIDTH - 1) // 2
RWKV_HEAD = 64
RWKV_HEADS = D // RWKV_HEAD
GN_EPS = 64e-5
N_GROUPS = 4
EXPERTS_PER_GROUP = 8
N_EXPERTS = N_GROUPS * EXPERTS_PER_GROUP
TOP_K = 2
EXPERT_FF = D // 2
LN_EPS = 1e-5
ALPHA = (2 * DEPTH) ** 0.25

LANES = 128
ROW_TILE = 256
MOE_ROWS = 512
ATT_Q = 128
SCAN_CHUNK = 64
SCAN_ROWS = 256
SCAN_PAIRS = 8
MOD_ROWS = 16
VMEM_LIMIT = 56 * 1024 * 1024
NEG = -1e30
HI = lax.Precision.HIGHEST


def _params(n_axes):
    return pltpu.CompilerParams(dimension_semantics=("arbitrary",) * n_axes,
                                vmem_limit_bytes=VMEM_LIMIT)


def _const_spec(shape):
    nd = len(shape)
    return pl.BlockSpec(shape, lambda *_: (0,) * nd)


class _Layout:
    def __init__(self, bp, sp, bs, ss):
        self.bp, self.sp, self.bs, self.ss = bp, sp, bs, ss
        self.n_prompt = bp * sp
        self.n = bp * sp + bs * ss
        assert sp % ROW_TILE == 0 and ss % ROW_TILE == 0
        self.n_tiles = self.n // ROW_TILE
        self.prompt_tiles = self.n_prompt // ROW_TILE
        self.tiles_per_sseq = ss // ROW_TILE
        self.tiles_per_pseq = sp // ROW_TILE

    def mod_row(self, i):
        return jnp.where(i < self.prompt_tiles, 0, 1 + (i - self.prompt_tiles) // self.tiles_per_sseq)

    def seq_pos(self, i):
        in_p = i < self.prompt_tiles
        pos = jnp.where(in_p, i % self.tiles_per_pseq, (i - self.prompt_tiles) % self.tiles_per_sseq)
        cnt = jnp.where(in_p, self.tiles_per_pseq, self.tiles_per_sseq)
        return pos, cnt

    def row_spec(self, width=D, col=0):
        return pl.BlockSpec((ROW_TILE, width), lambda i: (i, col))

    def mod_spec(self):
        return pl.BlockSpec((None, 1, D), lambda i: (self.mod_row(i), 0, 0))


def _layer_norm(z, g, b, eps=LN_EPS):
    mu = jnp.mean(z, axis=-1, keepdims=True)
    zc = z - mu
    var = jnp.mean(zc * zc, axis=-1, keepdims=True)
    return zc * lax.rsqrt(var + eps) * g + b


def _silu(x):
    return x * jax.nn.sigmoid(x)


def _mod_body(c_ref, w_ref, b_ref, o_ref):
    o_ref[...] = jnp.dot(_silu(c_ref[...]), w_ref[...], precision=HI,
                         preferred_element_type=f32) + b_ref[...]


def _modulation_table(cond, mod_w, mod_b):
    out = pl.pallas_call(
        _mod_body,
        grid=(DEPTH, 6),
        in_specs=[_const_spec((MOD_ROWS, D)),
                  pl.BlockSpec((None, D, D), lambda l, j: (l, 0, j)),
                  pl.BlockSpec((None, 1, D), lambda l, j: (l, 0, j))],
        out_specs=pl.BlockSpec((None, None, MOD_ROWS, D), lambda l, j: (l, j, 0, 0)),
        out_shape=jax.ShapeDtypeStruct((DEPTH, 6, MOD_ROWS, D), f32),
        compiler_params=_params(2),
        name="modulation",
    )(cond, mod_w, mod_b.reshape(DEPTH, 1, 6 * D))
    return out.reshape(DEPTH, 6, MOD_ROWS, 1, D)


def _proj_body(x_ref, sh_ref, sc_ref, w_ref, b_ref, o_ref, *, glu):
    h = x_ref[...] * (1.0 + sc_ref[...]) + sh_ref[...]
    u = jnp.dot(h.astype(bf16), w_ref[...], preferred_element_type=f32) + b_ref[...]
    if glu:
        half = u.shape[1] // 2
        u = u[:, :half] * jax.nn.sigmoid(u[:, half:])
    o_ref[...] = u.astype(o_ref.dtype)


def _mod_proj(lay, x, shift, scale, w, b, *, glu=False, out_dtype=f32, name="proj"):
    k, n = w.shape
    n_out = n // 2 if glu else n
    return pl.pallas_call(
        functools.partial(_proj_body, glu=glu),
        grid=(lay.n_tiles,),
        in_specs=[lay.row_spec(), lay.mod_spec(), lay.mod_spec(),
                  _const_spec((k, n)), _const_spec((1, n))],
        out_specs=lay.row_spec(n_out),
        out_shape=jax.ShapeDtypeStruct((lay.n, n_out), out_dtype),
        compiler_params=_params(1),
        name=name,
    )(x, shift, scale, w.astype(bf16), b.reshape(1, n))


def _out_ln_body(y_ref, w_ref, b_ref, x_ref, gate_ref, g_ref, be_ref, o_ref):
    t = jnp.dot(y_ref[...].astype(bf16), w_ref[...], preferred_element_type=f32) + b_ref[...]
    z = ALPHA * x_ref[...] + gate_ref[...] * t
    o_ref[...] = _layer_norm(z, g_ref[...], be_ref[...])


def _out_proj_ln(lay, y, w, b, x, gate, ln_g, ln_b, name="out_ln"):
    k = w.shape[0]
    return pl.pallas_call(
        _out_ln_body,
        grid=(lay.n_tiles,),
        in_specs=[lay.row_spec(k), _const_spec((k, D)), _const_spec((1, D)),
                  lay.row_spec(), lay.mod_spec(), _const_spec((1, D)), _const_spec((1, D))],
        out_specs=lay.row_spec(),
        out_shape=jax.ShapeDtypeStruct((lay.n, D), f32),
        compiler_params=_params(1),
        name=name,
    )(y, w.astype(bf16), b.reshape(1, D), x, gate, ln_g.reshape(1, D), ln_b.reshape(1, D))


def _sink_column(sink_ref, h, rows_per_head):
    r = lax.broadcasted_iota(jnp.int32, (Q_PER_KV * rows_per_head, 1), 0) // rows_per_head
    col = jnp.full(r.shape, sink_ref[h * Q_PER_KV], f32)
    for g in range(1, Q_PER_KV):
        col = jnp.where(r == g, sink_ref[h * Q_PER_KV + g], col)
    return col


def _stack_heads(q, h):
    return jnp.concatenate(
        [q[:, (h * Q_PER_KV + g) * HEAD_DIM:(h * Q_PER_KV + g + 1) * HEAD_DIM] for g in range(Q_PER_KV)], axis=0)


def _nt(a, b):
    return lax.dot_general(a, b, (((1,), (1,)), ((), ())), preferred_element_type=f32)


def _ctx_attn_body(sink_ref, q_ref, k_ref, v_ref, o_ref):
    rows = q_ref.shape[0]
    q = q_ref[...] * (HEAD_DIM ** -0.5)
    k = k_ref[...]
    v = v_ref[...]
    outs = [None] * N_Q_HEADS
    for h in range(N_KV_HEADS):
        qh = _stack_heads(q, h).astype(bf16)
        kh = k[:, h * HEAD_DIM:(h + 1) * HEAD_DIM].astype(bf16)
        vh = v[:, h * HEAD_DIM:(h + 1) * HEAD_DIM].astype(bf16)
        s = _nt(qh, kh)
        sk = _sink_column(sink_ref, h, rows)
        m = jnp.maximum(jnp.max(s, axis=-1, keepdims=True), sk)
        p = jnp.exp(s - m)
        den = jnp.sum(p, axis=-1, keepdims=True) + jnp.exp(sk - m)
        o = jnp.dot(p.astype(bf16), vh, preferred_element_type=f32) / den
        for g in range(Q_PER_KV):
            outs[h * Q_PER_KV + g] = o[g * rows:(g + 1) * rows]
    o_ref[...] = jnp.concatenate(outs, axis=1).astype(o_ref.dtype)


def _context_attention(lay, qkv, sink):
    sp = lay.sp
    kcol = Q_DIM // KV_DIM
    return pl.pallas_call(
        _ctx_attn_body,
        grid=(lay.bp,),
        in_specs=[pl.BlockSpec(memory_space=pltpu.SMEM),
                  pl.BlockSpec((sp, Q_DIM), lambda b: (b, 0)),
                  pl.BlockSpec((sp, KV_DIM), lambda b: (b, kcol)),
                  pl.BlockSpec((sp, KV_DIM), lambda b: (b, kcol + 1))],
        out_specs=pl.BlockSpec((sp, Q_DIM), lambda b: (b, 0)),
        out_shape=jax.ShapeDtypeStruct((lay.n_prompt, Q_DIM), bf16),
        compiler_params=_params(1),
        name="ctx_attn",
    )(sink, qkv, qkv, qkv)


def _rope_tables(t_len):
    t = jnp.arange(t_len)
    row = (t // GRID_W).astype(f32)
    col = (t % GRID_W).astype(f32)
    quarter = HEAD_DIM // 4
    d = jnp.arange(LANES) % HEAD_DIM
    inv = ROPE_BASE ** (-(d % quarter).astype(f32) / quarter)
    pos = jnp.where((d < HEAD_DIM // 2)[None, :], row[:, None], col[:, None])
    ang = pos * inv[None, :]
    cs, sn = jnp.cos(ang), jnp.sin(ang)
    first = ((d % (2 * quarter)) < quarter)[None, :]
    return cs, jnp.where(first, -sn, 0.0), jnp.where(first, 0.0, sn)


def _rope(x, cs, sa, sb):
    quarter = HEAD_DIM // 4
    outs = []
    for c in range(x.shape[1] // LANES):
        xc = x[:, c * LANES:(c + 1) * LANES]
        outs.append(xc * cs + pltpu.roll(xc, LANES - quarter, axis=1) * sa + pltpu.roll(xc, quarter, axis=1) * sb)
    return outs[0] if len(outs) == 1 else jnp.concatenate(outs, axis=1)


def _lat_attn_body(sink_ref, q_ref, k_ref, v_ref, kc_ref, vc_ref, cs_ref, sa_ref, sb_ref, o_ref):
    j = pl.program_id(1)
    t_len = k_ref.shape[0]
    win = 3 * ATT_Q
    start = pl.multiple_of(jnp.clip((j - 1) * ATT_Q, 0, t_len - win), ATT_Q)
    q0 = pl.multiple_of(j * ATT_Q, ATT_Q)
    q = _rope(q_ref[...], cs_ref[pl.ds(q0, ATT_Q), :], sa_ref[pl.ds(q0, ATT_Q), :],
              sb_ref[pl.ds(q0, ATT_Q), :]) * (HEAD_DIM ** -0.5)
    kw = _rope(k_ref[pl.ds(start, win), :], cs_ref[pl.ds(start, win), :], sa_ref[pl.ds(start, win), :],
               sb_ref[pl.ds(start, win), :])
    vw = v_ref[pl.ds(start, win), :]
    kc = kc_ref[...]
    vc = vc_ref[...]
    rows = Q_PER_KV * ATT_Q
    qpos = q0 + lax.broadcasted_iota(jnp.int32, (rows, win), 0) % ATT_Q
    kpos = start + lax.broadcasted_iota(jnp.int32, (rows, win), 1)
    valid = jnp.abs(qpos - kpos) <= WINDOW
    outs = [None] * N_Q_HEADS
    for h in range(N_KV_HEADS):
        sl = slice(h * HEAD_DIM, (h + 1) * HEAD_DIM)
        qh = _stack_heads(q, h).astype(bf16)
        s_c = _nt(qh, kc[:, sl].astype(bf16))
        s_l = jnp.where(valid, _nt(qh, kw[:, sl].astype(bf16)), NEG)
        sk = _sink_column(sink_ref, h, ATT_Q)
        m = jnp.maximum(jnp.maximum(jnp.max(s_c, axis=-1, keepdims=True),
                                    jnp.max(s_l, axis=-1, keepdims=True)), sk)
        p_c = jnp.exp(s_c - m)
        p_l = jnp.exp(s_l - m)
        den = jnp.sum(p_c, axis=-1, keepdims=True) + jnp.sum(p_l, axis=-1, keepdims=True) + jnp.exp(sk - m)
        o = (jnp.dot(p_c.astype(bf16), vc[:, sl].astype(bf16), preferred_element_type=f32)
             + jnp.dot(p_l.astype(bf16), vw[:, sl].astype(bf16), preferred_element_type=f32)) / den
        for g in range(Q_PER_KV):
            outs[h * Q_PER_KV + g] = o[g * ATT_Q:(g + 1) * ATT_Q]
    o_ref[...] = jnp.concatenate(outs, axis=1).astype(o_ref.dtype)


def _latent_attention(lay, qkv, kc, vc, sink):
    ss = lay.ss
    past = kc.shape[1]
    nq = ss // ATT_Q
    q_base = lay.n_prompt // ATT_Q
    s_base = lay.n_prompt // ss
    assert lay.n_prompt % ss == 0
    kcol = Q_DIM // KV_DIM
    cs, sa, sb = _rope_tables(ss)
    return pl.pallas_call(
        _lat_attn_body,
        grid=(lay.bs, nq),
        in_specs=[pl.BlockSpec(memory_space=pltpu.SMEM),
                  pl.BlockSpec((ATT_Q, Q_DIM), lambda b, j: (q_base + b * nq + j, 0)),
                  pl.BlockSpec((ss, KV_DIM), lambda b, j: (s_base + b, kcol)),
                  pl.BlockSpec((ss, KV_DIM), lambda b, j: (s_base + b, kcol + 1)),
                  pl.BlockSpec((None, past, KV_DIM), lambda b, j: (b, 0, 0)),
                  pl.BlockSpec((None, past, KV_DIM), lambda b, j: (b, 0, 0)),
                  _const_spec((ss, LANES)), _const_spec((ss, LANES)), _const_spec((ss, LANES))],
        out_specs=pl.BlockSpec((ATT_Q, Q_DIM), lambda b, j: (b * nq + j, 0)),
        out_shape=jax.ShapeDtypeStruct((lay.bs * ss, Q_DIM), bf16),
        compiler_params=_params(2),
        name="lat_attn",
    )(sink, qkv, qkv, qkv, kc, vc, cs, sa, sb)


CONV_HALO = 16
CONV_SUB = 32


def _dwconv_body(lay, u_ref, prev_ref, next_ref, w_ref, b_ref, g_ref, be_ref, o_ref, buf_ref):
    i = pl.program_id(0)
    pos, cnt = lay.seq_pos(i)
    buf_ref[pl.ds(0, CONV_HALO), :] = jnp.where(pos > 0, prev_ref[...], 0.0)
    buf_ref[pl.ds(CONV_HALO, ROW_TILE), :] = u_ref[...]
    buf_ref[pl.ds(CONV_HALO + ROW_TILE, CONV_HALO), :] = jnp.where(pos < cnt - 1, next_ref[...], 0.0)
    w = w_ref[...]
    off = CONV_HALO - CONV_PAD

    for s in range(ROW_TILE // CONV_SUB):
        r0 = s * CONV_SUB
        acc = jnp.zeros((CONV_SUB, D), f32)
        for t in range(CONV_WIDTH):
            acc = acc + buf_ref[pl.ds(r0 + off + t, CONV_SUB), :] * w[t:t + 1, :]
        z = _layer_norm(acc + b_ref[...], g_ref[...], be_ref[...])
        o_ref[pl.ds(r0, CONV_SUB), :] = _silu(z).astype(o_ref.dtype)


def _dwconv_ln_swish(lay, u, w_dw, b_dw, ln_g, ln_b):
    hpt = ROW_TILE // CONV_HALO
    last = lay.n // CONV_HALO - 1
    return pl.pallas_call(
        functools.partial(_dwconv_body, lay),
        grid=(lay.n_tiles,),
        in_specs=[lay.row_spec(),
                  pl.BlockSpec((CONV_HALO, D), lambda i: (jnp.maximum(i * hpt - 1, 0), 0)),
                  pl.BlockSpec((CONV_HALO, D), lambda i: (jnp.minimum((i + 1) * hpt, last), 0)),
                  _const_spec((CONV_WIDTH, D)), _const_spec((1, D)), _const_spec((1, D)), _const_spec((1, D))],
        out_specs=lay.row_spec(),
        out_shape=jax.ShapeDtypeStruct((lay.n, D), bf16),
        scratch_shapes=[pltpu.VMEM((ROW_TILE + 2 * CONV_HALO, D), f32)],
        compiler_params=_params(1),
        name="dwconv",
    )(u, u, u, w_dw, b_dw.reshape(1, D), ln_g.reshape(1, D), ln_b.reshape(1, D))


def _conformer(lay, x, m, w_pw1, b_pw1, w_dw, b_dw, ln_g, ln_b, w_pw2, b_pw2):
    u = _mod_proj(lay, x, m[0], m[1], w_pw1, b_pw1, glu=True, name="conv_pw1_glu")
    return _dwconv_ln_swish(lay, u, w_dw, b_dw, ln_g, ln_b)


LORA_PAD = 128
GATE_PAD = 256


def _bdot(a, w):
    return jnp.dot(a.astype(bf16), w, preferred_element_type=f32)


def _rwkv_proj_body(lay, x_ref, prev_ref, next_ref, sh_ref, sc_ref, mu_ref, wrkv_ref, w0_ref, w1_ref, w2_ref,
                    a0_ref, a1_ref, a2_ref, g1_ref, g2_ref, r_ref, k_ref, v_ref, g_ref, lw_ref, a_ref):
    i = pl.program_id(0)
    pos, cnt = lay.seq_pos(i)
    sc = 1.0 + sc_ref[...]
    sh = sh_ref[...]
    h = x_ref[...] * sc + sh
    hb = prev_ref.shape[0]
    h_prev = jnp.where(pos > 0, prev_ref[hb - 1:hb, :] * sc + sh, 0.0)
    h_next = jnp.where(pos < cnt - 1, next_ref[0:1, :] * sc + sh, 0.0)
    row = lax.broadcasted_iota(jnp.int32, (ROW_TILE, 1), 0)
    below = jnp.where(row == 0, h_prev, pltpu.roll(h, 1, axis=0))
    above = jnp.where(row == ROW_TILE - 1, h_next, pltpu.roll(h, ROW_TILE - 1, axis=0))
    xx = 0.5 * (below + above) - h

    def mix(j):
        return (h + xx * mu_ref[j:j + 1, :]).astype(bf16)

    r_ref[...] = _bdot(mix(0), wrkv_ref[0])
    k_ref[...] = _bdot(mix(2), wrkv_ref[1])
    v_ref[...] = _bdot(mix(3), wrkv_ref[2])
    g_ref[...] = _bdot(jax.nn.sigmoid(_bdot(mix(5), g1_ref[...])), g2_ref[...])
    xw = mix(1)
    xa = mix(4)
    for d in range(2):
        z = w0_ref[d:d + 1, :] + _bdot(jnp.tanh(_bdot(xw, w1_ref[d])), w2_ref[d])
        softplus = jnp.maximum(-z, 0.0) + jnp.log(1.0 + jnp.exp(-jnp.abs(z)))
        lw_ref[d] = -jnp.exp(-softplus - 0.5)
        a_ref[d] = jax.nn.sigmoid(a0_ref[d:d + 1, :] + _bdot(_bdot(xa, a1_ref[d]), a2_ref[d]))


def _pad_to(w, axis, size):
    pad = [(0, 0)] * w.ndim
    pad[axis] = (0, size - w.shape[axis])
    return jnp.pad(w, pad)


def _rwkv_project(lay, x, m, mu, w_rkv, w0, w1, w2, a0, a1, a2, g1, g2):
    halo = 8
    hpt = ROW_TILE // halo
    last = lay.n // halo - 1
    nd = (lay.n, D)
    consts = [mu, w_rkv.astype(bf16), w0,
              _pad_to(w1, 2, LORA_PAD).astype(bf16), _pad_to(w2, 1, LORA_PAD).astype(bf16), a0,
              _pad_to(a1, 2, LORA_PAD).astype(bf16), _pad_to(a2, 1, LORA_PAD).astype(bf16),
              _pad_to(g1, 1, GATE_PAD).astype(bf16), _pad_to(g2, 0, GATE_PAD).astype(bf16)]
    return pl.pallas_call(
        functools.partial(_rwkv_proj_body, lay),
        grid=(lay.n_tiles,),
        in_specs=[lay.row_spec(),
                  pl.BlockSpec((halo, D), lambda i: (jnp.maximum(i * hpt - 1, 0), 0)),
                  pl.BlockSpec((halo, D), lambda i: (jnp.minimum((i + 1) * hpt, last), 0)),
                  lay.mod_spec(), lay.mod_spec()] + [_const_spec(c.shape) for c in consts],
        out_specs=[lay.row_spec()] * 4 + [pl.BlockSpec((2, ROW_TILE, D), lambda i: (0, i, 0))] * 2,
        out_shape=[jax.ShapeDtypeStruct(nd, f32)] * 4 + [jax.ShapeDtypeStruct((2,) + nd, f32)] * 2,
        compiler_params=_params(1),
        name="rwkv_proj",
    )(x, x, x, m[0], m[1], *consts)


def _split_dot(x, c, parts):
    acc = None
    for _ in range(parts):
        hi = x.astype(bf16)
        t = jnp.dot(hi, c, preferred_element_type=f32)
        acc = t if acc is None else acc + t
        x = x - hi.astype(f32)
    return acc


def _split_dot_left(c, x, parts):
    acc = None
    for _ in range(parts):
        hi = x.astype(bf16)
        t = jnp.dot(c, hi, preferred_element_type=f32)
        acc = t if acc is None else acc + t
        x = x - hi.astype(f32)
    return acc


def _mm(a, b):
    return jnp.dot(a.astype(bf16), b.astype(bf16), preferred_element_type=f32)


def _split3(x):
    x1 = x.astype(bf16)
    r1 = x - x1.astype(f32)
    x2 = r1.astype(bf16)
    x3 = (r1 - x2.astype(f32)).astype(bf16)
    return x1, x2, x3


def _mm_f32(a3, b):
    b1, b2, b3 = _split3(b)
    n = a3[0].shape[0]
    r1 = jnp.dot(jnp.concatenate(a3, axis=0), b1, preferred_element_type=f32)
    r2 = jnp.dot(jnp.concatenate(a3[:2], axis=0), b2, preferred_element_type=f32)
    r3 = jnp.dot(a3[0], b3, preferred_element_type=f32)
    small = r1[2 * n:] + r2[n:] + r3
    return r1[:n] + ((r1[n:2 * n] + r2[:n]) + small)


def _mm_nt(a, b):
    return lax.dot_general(a.astype(bf16), b.astype(bf16), (((1,), (1,)), ((), ())), preferred_element_type=f32)


def _scan_body(lay, rev, r_ref, k_ref, v_ref, lw_ref, a_ref, kk_ref, ka_ref, rk_ref, s0_ref, tri_ref, bd_ref,
               y_ref, bon_ref, sfin_ref, state_ref):
    t_idx = pl.program_id(1)
    i = (lay.n_tiles - 1 - t_idx) if rev else t_idx
    pos, cnt = lay.seq_pos(i)
    first = (pos == cnt - 1) if rev else (pos == 0)
    final = (pos == 0) if rev else (pos == cnt - 1)
    c_len = SCAN_CHUNK
    n_chunks = SCAN_ROWS // c_len

    @pl.when(first)
    def _():
        state_ref[...] = s0_ref[...]

    lane = lax.broadcasted_iota(jnp.int32, (1, LANES), 1)
    head0 = lane < RWKV_HEAD
    rr = lax.broadcasted_iota(jnp.int32, (2 * c_len, 2 * c_len), 0)
    cc = lax.broadcasted_iota(jnp.int32, (2 * c_len, 2 * c_len), 1)
    same = (rr // c_len) == (cc // c_len)
    tt = rr % c_len
    ss = cc % c_len
    strict = same & ((tt < ss) if rev else (tt > ss))
    incl = same & ((tt <= ss) if rev else (tt >= ss))
    tri = tri_ref[...]
    bd = bd_ref[...]

    def stack(xc):
        return jnp.concatenate([jnp.where(head0, xc, 0.0), jnp.where(head0, 0.0, xc)], axis=0)

    def pair_chunk(rows, p):
        cols = slice(p * LANES, (p + 1) * LANES)
        r = r_ref[rows, cols]
        k = k_ref[rows, cols]
        v = v_ref[rows, cols]
        lw = lw_ref[rows, cols]
        a = a_ref[rows, cols]
        kk = k * kk_ref[:, cols]
        nrm = jnp.sqrt(_split_dot(kk * kk, bd, 3))
        kk = kk / jnp.maximum(nrm, 1e-12)
        b = kk * a
        kd = k * (1.0 + (a - 1.0) * ka_ref[:, cols])
        bon_ref[rows, cols] = _split_dot(r * kd * rk_ref[:, cols], bd, 3) * v
        cum = _split_dot_left(tri, lw, 3)
        tot = cum[0:1, :] if rev else cum[c_len - 1:c_len, :]
        e_tot = jnp.exp(tot)
        e_neg = jnp.exp(-cum)
        a_t = -kk * jnp.exp(cum - lw)
        r_t = r * jnp.exp(cum)
        b_t = b * e_neg
        k_t = kd * e_neg
        ar = jnp.concatenate([stack(a_t), stack(r_t)], axis=0)
        bk = jnp.concatenate([stack(b_t), stack(k_t)], axis=0)
        m = _mm_nt(ar, bk)
        h2 = 2 * c_len
        n_ab = jnp.where(strict, m[:h2, :h2], 0.0)
        a_ak = jnp.where(strict, m[:h2, h2:], 0.0)
        a_rb = jnp.where(incl, m[h2:, :h2], 0.0)
        a_rk = jnp.where(incl, m[h2:, h2:], 0.0)
        state = state_ref[p]
        x0 = _mm_nt(ar, state)
        v_st = stack(v)
        sa = x0[:h2] + _mm(a_ak, v_st)
        pw = n_ab
        span = 1
        while span < c_len:
            pw3 = _split3(pw)
            sa = sa + _mm_f32(pw3, sa)
            span *= 2
            if span < c_len:
                pw = _mm_f32(pw3, pw)
        sav = jnp.concatenate([sa, v_st], axis=0)
        y_st = x0[h2:] + _mm(jnp.concatenate([a_rb, a_rk], axis=1), sav)
        y_ref[rows, cols] = y_st[:c_len] + y_st[c_len:]
        state_ref[p] = state * e_tot + _mm(sav.T, bk * e_tot)

    def chunk(c, carry):
        ce = (n_chunks - 1 - c) if rev else c
        rows = pl.ds(pl.multiple_of(ce * c_len, c_len), c_len)
        for p in range(SCAN_PAIRS):
            pair_chunk(rows, p)
        return carry

    lax.fori_loop(0, n_chunks, chunk, 0)

    @pl.when(final)
    def _():
        sfin_ref[...] = state_ref[...]


def _block_diag_states(s):
    b = s.shape[0]
    s = s.reshape(b, RWKV_HEADS // 2, 2, RWKV_HEAD, RWKV_HEAD)
    eye = jnp.eye(2, dtype=s.dtype)
    out = s[:, :, :, :, None, :] * eye[None, None, :, None, :, None]
    return out.reshape(b, RWKV_HEADS // 2, LANES, LANES)


def _diag_blocks(w):
    b = w.shape[0]
    w = w.reshape(b, RWKV_HEADS // 2, 2, RWKV_HEAD, 2, RWKV_HEAD)
    return jnp.stack([w[:, :, 0, :, 0, :], w[:, :, 1, :, 1, :]], axis=2).reshape(b, RWKV_HEADS, RWKV_HEAD, RWKV_HEAD)


def _rwkv_scan(lay, rev, r, k, v, lw, a, k_k, k_a, r_k, s0):
    d = 1 if rev else 0
    n_seq = lay.bp + lay.bs
    c_len = SCAN_CHUNK
    idx = jnp.arange(c_len)
    tri = ((idx[:, None] <= idx[None, :]) if rev else (idx[:, None] >= idx[None, :])).astype(bf16)
    hh = jnp.arange(LANES) // RWKV_HEAD
    bd = (hh[:, None] == hh[None, :]).astype(bf16)

    def tile(t):
        return (lay.n_tiles - 1 - t) if rev else t

    def seq(t):
        i = tile(t)
        return jnp.where(i < lay.prompt_tiles, i // lay.tiles_per_pseq,
                         lay.bp + (i - lay.prompt_tiles) // lay.tiles_per_sseq)

    width = SCAN_PAIRS * LANES
    row = pl.BlockSpec((SCAN_ROWS, width), lambda p, t: (tile(t), p))
    row_d = pl.BlockSpec((None, SCAN_ROWS, width), lambda p, t: (d, tile(t), p))
    par = pl.BlockSpec((1, width), lambda p, t: (0, p))
    st = pl.BlockSpec((None, SCAN_PAIRS, LANES, LANES), lambda p, t: (seq(t), p, 0, 0))
    return pl.pallas_call(
        functools.partial(_scan_body, lay, rev),
        grid=(RWKV_HEADS // 2 // SCAN_PAIRS, lay.n_tiles),
        in_specs=[row, row, row, row_d, row_d, par, par, par, st,
                  _const_spec((c_len, c_len)), _const_spec((LANES, LANES))],
        out_specs=[row, row, st],
        out_shape=[jax.ShapeDtypeStruct((lay.n, D), f32), jax.ShapeDtypeStruct((lay.n, D), f32),
                   jax.ShapeDtypeStruct((n_seq, RWKV_HEADS // 2, LANES, LANES), f32)],
        scratch_shapes=[pltpu.VMEM((SCAN_PAIRS, LANES, LANES), f32)],
        compiler_params=_params(2),
        name="rwkv_scan_bwd" if rev else "rwkv_scan_fwd",
    )(r, k, v, lw, a, k_k.reshape(1, D), k_a.reshape(1, D), r_k.reshape(1, D), s0, tri, bd)


def _rwkv_out_body(yf_ref, yb_ref, bf_ref, bb_ref, g_ref, gg_ref, gb_ref, bd_ref, o_ref):
    y = yf_ref[...] + yb_ref[...]
    bd = bd_ref[...]
    inv = 1.0 / RWKV_HEAD
    outs = []
    for c in range(D // LANES):
        yc = y[:, c * LANES:(c + 1) * LANES]
        cen = yc - _split_dot(yc, bd, 3) * inv
        var = _split_dot(cen * cen, bd, 3) * inv
        outs.append(cen * lax.rsqrt(var + GN_EPS))
    yn = jnp.concatenate(outs, axis=1) * gg_ref[...] + gb_ref[...]
    o_ref[...] = ((yn + bf_ref[...] + bb_ref[...]) * g_ref[...]).astype(o_ref.dtype)


def _rwkv(lay, x, m, s0_f, s0_b, mu, w_rkv, w0, w1, w2, a0, a1, a2, g1, g2, k_k, k_a, r_k, gn_g, gn_b):
    r, k, v, g, lw, a = _rwkv_project(lay, x, m, mu, w_rkv, w0, w1, w2, a0, a1, a2, g1, g2)
    zeros = jnp.zeros((lay.bp, RWKV_HEADS // 2, LANES, LANES), f32)
    outs = []
    for rev, s0 in ((False, s0_f), (True, s0_b)):
        s_all = jnp.concatenate([zeros, _block_diag_states(s0.astype(f32))], axis=0)
        outs.append(_rwkv_scan(lay, rev, r, k, v, lw, a, k_k, k_a, r_k, s_all))
    hh = jnp.arange(LANES) // RWKV_HEAD
    bd = (hh[:, None] == hh[None, :]).astype(bf16)
    y = pl.pallas_call(
        _rwkv_out_body,
        grid=(lay.n_tiles,),
        in_specs=[lay.row_spec()] * 5 + [_const_spec((1, D)), _const_spec((1, D)), _const_spec((LANES, LANES))],
        out_specs=lay.row_spec(),
        out_shape=jax.ShapeDtypeStruct((lay.n, D), bf16),
        compiler_params=_params(1),
        name="rwkv_groupnorm_gate",
    )(outs[0][0], outs[1][0], outs[0][1], outs[1][1], g, gn_g.reshape(1, D), gn_b.reshape(1, D), bd)
    return y, _diag_blocks(outs[0][2][:lay.bp]), _diag_blocks(outs[1][2][:lay.bp])


def _router_body(x_ref, sh_ref, sc_ref, w_ref, wlo_ref, b_ref, h_ref, route_ref):
    h = x_ref[...] * (1.0 + sc_ref[...]) + sh_ref[...]
    h_hi = h.astype(bf16)
    h_ref[...] = h_hi
    h_lo = (h - h_hi.astype(f32)).astype(bf16)
    lg = (jnp.dot(h_hi, w_ref[...], preferred_element_type=f32)
          + (jnp.dot(h_hi, wlo_ref[...], preferred_element_type=f32)
             + jnp.dot(h_lo, w_ref[...], preferred_element_type=f32))) + b_ref[...]
    lane = lax.broadcasted_iota(jnp.int32, lg.shape, 1)

    def top1(vals):
        mx = jnp.max(vals, axis=-1, keepdims=True)
        idx = jnp.min(jnp.where(vals == mx, lane, LANES), axis=-1, keepdims=True)
        return mx, idx

    gl = jnp.where(lane < N_GROUPS, lg, NEG)
    g_max, g_idx = top1(gl)
    p_g = 1.0 / jnp.sum(jnp.exp(gl - g_max), axis=-1, keepdims=True)
    lo = N_GROUPS + g_idx * EXPERTS_PER_GROUP
    el = jnp.where((lane >= lo) & (lane < lo + EXPERTS_PER_GROUP), lg, NEG)
    m1, i1 = top1(el)
    m2, i2 = top1(jnp.where(lane == i1, NEG, el))
    e2 = jnp.exp(m2 - m1)
    w1 = p_g / (1.0 + e2)
    w2 = p_g * e2 / (1.0 + e2)
    route = jnp.where(lane == 0, (i1 - N_GROUPS).astype(f32),
                      jnp.where(lane == 1, (i2 - N_GROUPS).astype(f32),
                                jnp.where(lane == 2, w1, jnp.where(lane == 3, w2, 0.0))))
    route_ref[...] = route


def _expert_body(be_ref, nu_ref, x_ref, wgu_ref, wd_ref, o_ref, wgu_s, wd_s):
    i = pl.program_id(0)
    used = i < nu_ref[0]

    @pl.when(used & ((i == 0) | (be_ref[i] != be_ref[jnp.maximum(i - 1, 0)])))
    def _():
        wgu_s[...] = wgu_ref[...].astype(bf16)
        wd_s[...] = wd_ref[...].astype(bf16)

    @pl.when(used)
    def _():
        u = jnp.dot(x_ref[...], wgu_s[...], preferred_element_type=f32)
        act = _silu(u[:, :EXPERT_FF]) * u[:, EXPERT_FF:]
        o_ref[...] = jnp.dot(act.astype(bf16), wd_s[...], preferred_element_type=f32).astype(o_ref.dtype)

    @pl.when(jnp.logical_not(used))
    def _():
        o_ref[...] = jnp.zeros_like(o_ref)


def _moe_out_body(y_ref0, y_ref1, route_ref, x_ref, gate_ref, g_ref, be_ref, o_ref):
    route = route_ref[...]
    y = route[:, 2:3] * y_ref0[...].astype(f32) + route[:, 3:4] * y_ref1[...].astype(f32)
    z = ALPHA * x_ref[...] + gate_ref[...] * y
    o_ref[...] = _layer_norm(z, g_ref[...], be_ref[...])


def _moe_layer(lay, x, m, w_group, b_group, w_expert, b_expert, w_gu, w_down, ln_g, ln_b, layer=None):
    if layer is None:
        w_gu, w_down, layer = w_gu[None], w_down[None], 0
    n = lay.n
    n_route = N_GROUPS + N_EXPERTS
    w_r = _pad_to(jnp.concatenate([w_group, w_expert], axis=1).astype(f32), 1, LANES)
    b_r = _pad_to(jnp.concatenate([b_group, b_expert]).astype(f32).reshape(1, n_route), 1, LANES)
    w_hi = w_r.astype(bf16)
    h, route = pl.pallas_call(
        _router_body,
        grid=(lay.n_tiles,),
        in_specs=[lay.row_spec(), lay.mod_spec(), lay.mod_spec(), _const_spec((D, LANES)), _const_spec((D, LANES)),
                  _const_spec((1, LANES))],
        out_specs=[lay.row_spec(), lay.row_spec(LANES)],
        out_shape=[jax.ShapeDtypeStruct((n, D), bf16), jax.ShapeDtypeStruct((n, LANES), f32)],
        compiler_params=_params(1),
        name="moe_router",
    )(x, m[3], m[4], w_hi, (w_r - w_hi.astype(f32)).astype(bf16), b_r)

    s = n * TOP_K
    e_flat = route[:, :TOP_K].astype(jnp.int32).reshape(-1)
    pair = jnp.arange(s, dtype=jnp.int32)
    e_sorted, order = lax.sort((e_flat, pair), num_keys=1, is_stable=True)
    experts = jnp.arange(N_EXPERTS, dtype=jnp.int32)
    counts = jnp.sum((e_flat[:, None] == experts[None, :]).astype(jnp.int32), axis=0)
    padded = (counts + MOE_ROWS - 1) // MOE_ROWS * MOE_ROWS
    pad_end = jnp.cumsum(padded)
    pad_start = pad_end - padded
    start = jnp.cumsum(counts) - counts
    dest_sorted = pad_start[e_sorted] + pair - start[e_sorted]
    n_blocks = -(-s // MOE_ROWS) + N_EXPERTS
    p_rows = n_blocks * MOE_ROWS
    block_start = jnp.arange(n_blocks, dtype=jnp.int32) * MOE_ROWS
    block_expert = jnp.minimum(jnp.sum((pad_end[None, :] <= block_start[:, None]).astype(jnp.int32), axis=1),
                               N_EXPERTS - 1)
    n_used = (pad_end[N_EXPERTS - 1] // MOE_ROWS).astype(jnp.int32).reshape(1)
    slot = jnp.arange(p_rows, dtype=jnp.int32)
    slot_expert = jnp.repeat(block_expert, MOE_ROWS)
    in_run = jnp.minimum(slot - pad_start[slot_expert], jnp.maximum(counts[slot_expert] - 1, 0))
    slot_token = order[jnp.clip(start[slot_expert] + in_run, 0, s - 1)] // TOP_K
    _, dest = lax.sort((order, dest_sorted), num_keys=1)
    dest = dest.reshape(n, TOP_K)

    xb = h.at[slot_token].get(mode="promise_in_bounds")
    yb = pl.pallas_call(
        _expert_body,
        grid_spec=pltpu.PrefetchScalarGridSpec(
            num_scalar_prefetch=2,
            grid=(n_blocks,),
            in_specs=[pl.BlockSpec((MOE_ROWS, D), lambda i, be, nu: (jnp.minimum(i, nu[0] - 1), 0)),
                      pl.BlockSpec((None, None, D, 2 * EXPERT_FF), lambda i, be, nu: (layer, be[i], 0, 0)),
                      pl.BlockSpec((None, None, EXPERT_FF, D), lambda i, be, nu: (layer, be[i], 0, 0))],
            out_specs=pl.BlockSpec((MOE_ROWS, D), lambda i, be, nu: (i, 0)),
            scratch_shapes=[pltpu.VMEM((D, 2 * EXPERT_FF), bf16), pltpu.VMEM((EXPERT_FF, D), bf16)]),
        out_shape=jax.ShapeDtypeStruct((p_rows, D), bf16),
        compiler_params=_params(1),
        name="moe_experts",
    )(block_expert, n_used, xb, w_gu, w_down)
    y0 = yb.at[dest[:, 0]].get(mode="promise_in_bounds")
    y1 = yb.at[dest[:, 1]].get(mode="promise_in_bounds")

    return pl.pallas_call(
        _moe_out_body,
        grid=(lay.n_tiles,),
        in_specs=[lay.row_spec(), lay.row_spec(), lay.row_spec(LANES), lay.row_spec(), lay.mod_spec(),
                  _const_spec((1, D)), _const_spec((1, D))],
        out_specs=lay.row_spec(),
        out_shape=jax.ShapeDtypeStruct((n, D), f32),
        compiler_params=_params(1),
        name="moe_combine_ln",
    )(y0, y1, route, x, m[5], ln_g.reshape(1, D), ln_b.reshape(1, D))


def kernel(x_prompt, x_sample, c, c_ctx, cache_attn_k, cache_attn_v, state_rwkv_fwd, state_rwkv_bwd, mod_w, mod_b, post_ln_g, post_ln_b, attn_w_qkv, attn_w_o, attn_sink, conv_w_pw1, conv_b_pw1, conv_w_dw, conv_b_dw, conv_ln_g, conv_ln_b, conv_w_pw2, conv_b_pw2, rwkv_mu, rwkv_w_rkv, rwkv_w0, rwkv_w1, rwkv_w2, rwkv_a0, rwkv_a1, rwkv_a2, rwkv_g1, rwkv_g2, rwkv_k_k, rwkv_k_a, rwkv_r_k, rwkv_gn_g, rwkv_gn_b, rwkv_w_o, moe_w_group, moe_b_group, moe_w_expert, moe_b_expert, moe_w_gate_up, moe_w_down):
    bp, sp, _ = x_prompt.shape
    bs, ss, _ = x_sample.shape
    assert 1 + bs <= MOD_ROWS
    lay = _Layout(bp, sp, bs, ss)
    x = jnp.concatenate([x_prompt.reshape(bp * sp, D), x_sample.reshape(bs * ss, D)], axis=0)
    cond = jnp.concatenate([c_ctx[None, :], c, jnp.zeros((MOD_ROWS - 1 - bs, D), f32)], axis=0)
    modt = _modulation_table(cond, mod_w, mod_b)
    zero_b = jnp.zeros((D,), f32)
    new_k, new_v, new_sf, new_sb = [], [], [], []
    n_attn = n_conv = n_rwkv = 0
    for i in range(DEPTH):
        m = modt[i]
        kind = i % 3
        if kind == 0:
            j = n_attn
            qkv = _mod_proj(lay, x, m[0], m[1], attn_w_qkv[j], jnp.zeros((Q_DIM + 2 * KV_DIM,), f32), name="attn_qkv")
            past = cache_attn_k.shape[2]
            o_p = _context_attention(lay, qkv, attn_sink[j])
            o_s = _latent_attention(lay, qkv, cache_attn_k[:, j].reshape(bs, past, KV_DIM),
                                    cache_attn_v[:, j].reshape(bs, past, KV_DIM), attn_sink[j])
            y = jnp.concatenate([o_p, o_s], axis=0)
            w_out, b_out = attn_w_o[j], zero_b
            new_k.append(qkv[:lay.n_prompt, Q_DIM:Q_DIM + KV_DIM].reshape(bp, sp, N_KV_HEADS, HEAD_DIM))
            new_v.append(qkv[:lay.n_prompt, Q_DIM + KV_DIM:].reshape(bp, sp, N_KV_HEADS, HEAD_DIM))
            n_attn += 1
        elif kind == 1:
            j = n_conv
            y = _conformer(lay, x, m, conv_w_pw1[j], conv_b_pw1[j], conv_w_dw[j], conv_b_dw[j], conv_ln_g[j],
                           conv_ln_b[j], conv_w_pw2[j], conv_b_pw2[j])
            w_out, b_out = conv_w_pw2[j], conv_b_pw2[j]
            n_conv += 1
        else:
            j = n_rwkv
            y, sf, sb = _rwkv(lay, x, m, state_rwkv_fwd[:, j], state_rwkv_bwd[:, j], rwkv_mu[j], rwkv_w_rkv[j],
                              rwkv_w0[j], rwkv_w1[j], rwkv_w2[j], rwkv_a0[j], rwkv_a1[j], rwkv_a2[j], rwkv_g1[j],
                              rwkv_g2[j], rwkv_k_k[j], rwkv_k_a[j], rwkv_r_k[j], rwkv_gn_g[j], rwkv_gn_b[j])
            w_out, b_out = rwkv_w_o[j], zero_b
            new_sf.append(sf)
            new_sb.append(sb)
            n_rwkv += 1
        x = _out_proj_ln(lay, y, w_out, b_out, x, m[2], post_ln_g[i, 0], post_ln_b[i, 0], name="mixer_out_ln")
        x = _moe_layer(lay, x, m, moe_w_group[i], moe_b_group[i], moe_w_expert[i], moe_b_expert[i],
                       moe_w_gate_up, moe_w_down, post_ln_g[i, 1], post_ln_b[i, 1], layer=i)
    y_prompt = x[:lay.n_prompt].reshape(bp, sp, D)
    y_sample = x[lay.n_prompt:].reshape(bs, ss, D)
    return (y_prompt, y_sample, jnp.stack(new_k, axis=1), jnp.stack(new_v, axis=1),
            jnp.stack(new_sf, axis=1).astype(x_prompt.dtype), jnp.stack(new_sb, axis=1).astype(x_prompt.dtype))
```

```python
import functools

import jax
import jax.numpy as jnp
from jax import lax
from jax.experimental import pallas as pl
from jax.experimental.pallas import tpu as pltpu

f32 = jnp.float32
bf16 = jnp.bfloat16

D = 1024
DEPTH = 4
GRID_W = 64
HEAD_DIM = 64
N_Q_HEADS = D // HEAD_DIM
N_KV_HEADS = N_Q_HEADS // 4
Q_PER_KV = N_Q_HEADS // N_KV_HEADS
Q_DIM = N_Q_HEADS * HEAD_DIM
KV_DIM = N_KV_HEADS * HEAD_DIM
WINDOW = 128
ROPE_BASE = 10000.0
CONV_WIDTH = 31
CONV_PAD = (CONV_WIDTH - 1) // 2
RWKV_HEAD = 64
RWKV_HEADS = D // RWKV_HEAD
GN_EPS = 64e-5
N_GROUPS = 4
EXPERTS_PER_GROUP = 8
N_EXPERTS = N_GROUPS * EXPERTS_PER_GROUP
TOP_K = 2
EXPERT_FF = D // 2
LN_EPS = 1e-5
ALPHA = (2 * DEPTH) ** 0.25

LANES = 128
ROW_TILE = 256
MOE_ROWS = 512
ATT_Q = 128
SCAN_CHUNK = 64
SCAN_ROWS = 256
SCAN_PAIRS = 8
MOD_ROWS = 16
VMEM_LIMIT = 56 * 1024 * 1024
NEG = -1e30
HI = lax.Precision.HIGHEST


def _params(n_axes):
    return pltpu.CompilerParams(dimension_semantics=("arbitrary",) * n_axes,
                                vmem_limit_bytes=VMEM_LIMIT)


def _const_spec(shape):
    nd = len(shape)
    return pl.BlockSpec(shape, lambda *_: (0,) * nd)


class _Layout:
    def __init__(self, bp, sp, bs, ss):
        self.bp, self.sp, self.bs, self.ss = bp, sp, bs, ss
        self.n_prompt = bp * sp
        self.n = bp * sp + bs * ss
        assert sp % ROW_TILE == 0 and ss % ROW_TILE == 0
        self.n_tiles = self.n // ROW_TILE
        self.prompt_tiles = self.n_prompt // ROW_TILE
        self.tiles_per_sseq = ss // ROW_TILE
        self.tiles_per_pseq = sp // ROW_TILE

    def mod_row(self, i):
        return jnp.where(i < self.prompt_tiles, 0, 1 + (i - self.prompt_tiles) // self.tiles_per_sseq)

    def seq_pos(self, i):
        in_p = i < self.prompt_tiles
        pos = jnp.where(in_p, i % self.tiles_per_pseq, (i - self.prompt_tiles) % self.tiles_per_sseq)
        cnt = jnp.where(in_p, self.tiles_per_pseq, self.tiles_per_sseq)
        return pos, cnt

    def row_spec(self, width=D, col=0):
        return pl.BlockSpec((ROW_TILE, width), lambda i: (i, col))

    def mod_spec(self):
        return pl.BlockSpec((None, 1, D), lambda i: (self.mod_row(i), 0, 0))


def _layer_norm(z, g, b, eps=LN_EPS):
    mu = jnp.mean(z, axis=-1, keepdims=True)
    zc = z - mu
    var = jnp.mean(zc * zc, axis=-1, keepdims=True)
    return zc * lax.rsqrt(var + eps) * g + b


def _silu(x):
    return x * jax.nn.sigmoid(x)


def _mod_body(c_ref, w_ref, b_ref, o_ref):
    o_ref[...] = jnp.dot(_silu(c_ref[...]), w_ref[...], precision=HI,
                         preferred_element_type=f32) + b_ref[...]


def _modulation_table(cond, mod_w, mod_b):
    out = pl.pallas_call(
        _mod_body,
        grid=(DEPTH, 6),
        in_specs=[_const_spec((MOD_ROWS, D)),
                  pl.BlockSpec((None, D, D), lambda l, j: (l, 0, j)),
                  pl.BlockSpec((None, 1, D), lambda l, j: (l, 0, j))],
        out_specs=pl.BlockSpec((None, None, MOD_ROWS, D), lambda l, j: (l, j, 0, 0)),
        out_shape=jax.ShapeDtypeStruct((DEPTH, 6, MOD_ROWS, D), f32),
        compiler_params=_params(2),
        name="modulation",
    )(cond, mod_w, mod_b.reshape(DEPTH, 1, 6 * D))
    return out.reshape(DEPTH, 6, MOD_ROWS, 1, D)


def _proj_body(x_ref, sh_ref, sc_ref, w_ref, b_ref, o_ref, *, glu):
    h = x_ref[...] * (1.0 + sc_ref[...]) + sh_ref[...]
    u = jnp.dot(h.astype(bf16), w_ref[...], preferred_element_type=f32) + b_ref[...]
    if glu:
        half = u.shape[1] // 2
        u = u[:, :half] * jax.nn.sigmoid(u[:, half:])
    o_ref[...] = u.astype(o_ref.dtype)


def _mod_proj(lay, x, shift, scale, w, b, *, glu=False, out_dtype=f32, name="proj"):
    k, n = w.shape
    n_out = n // 2 if glu else n
    return pl.pallas_call(
        functools.partial(_proj_body, glu=glu),
        grid=(lay.n_tiles,),
        in_specs=[lay.row_spec(), lay.mod_spec(), lay.mod_spec(),
                  _const_spec((k, n)), _const_spec((1, n))],
        out_specs=lay.row_spec(n_out),
        out_shape=jax.ShapeDtypeStruct((lay.n, n_out), out_dtype),
        compiler_params=_params(1),
        name=name,
    )(x, shift, scale, w.astype(bf16), b.reshape(1, n))


def _out_ln_body(y_ref, w_ref, b_ref, x_ref, gate_ref, g_ref, be_ref, o_ref):
    t = jnp.dot(y_ref[...].astype(bf16), w_ref[...], preferred_element_type=f32) + b_ref[...]
    z = ALPHA * x_ref[...] + gate_ref[...] * t
    o_ref[...] = _layer_norm(z, g_ref[...], be_ref[...])


def _out_proj_ln(lay, y, w, b, x, gate, ln_g, ln_b, name="out_ln"):
    k = w.shape[0]
    return pl.pallas_call(
        _out_ln_body,
        grid=(lay.n_tiles,),
        in_specs=[lay.row_spec(k), _const_spec((k, D)), _const_spec((1, D)),
                  lay.row_spec(), lay.mod_spec(), _const_spec((1, D)), _const_spec((1, D))],
        out_specs=lay.row_spec(),
        out_shape=jax.ShapeDtypeStruct((lay.n, D), f32),
        compiler_params=_params(1),
        name=name,
    )(y, w.astype(bf16), b.reshape(1, D), x, gate, ln_g.reshape(1, D), ln_b.reshape(1, D))


def _sink_column(sink_ref, h, rows_per_head):
    r = lax.broadcasted_iota(jnp.int32, (Q_PER_KV * rows_per_head, 1), 0) // rows_per_head
    col = jnp.full(r.shape, sink_ref[h * Q_PER_KV], f32)
    for g in range(1, Q_PER_KV):
        col = jnp.where(r == g, sink_ref[h * Q_PER_KV + g], col)
    return col


def _stack_heads(q, h):
    return jnp.concatenate(
        [q[:, (h * Q_PER_KV + g) * HEAD_DIM:(h * Q_PER_KV + g + 1) * HEAD_DIM] for g in range(Q_PER_KV)], axis=0)


def _nt(a, b):
    return lax.dot_general(a, b, (((1,), (1,)), ((), ())), preferred_element_type=f32)


def _ctx_attn_body(sink_ref, q_ref, k_ref, v_ref, o_ref):
    rows = q_ref.shape[0]
    q = q_ref[...] * (HEAD_DIM ** -0.5)
    k = k_ref[...]
    v = v_ref[...]
    outs = [None] * N_Q_HEADS
    for h in range(N_KV_HEADS):
        qh = _stack_heads(q, h).astype(bf16)
        kh = k[:, h * HEAD_DIM:(h + 1) * HEAD_DIM].astype(bf16)
        vh = v[:, h * HEAD_DIM:(h + 1) * HEAD_DIM].astype(bf16)
        s = _nt(qh, kh)
        sk = _sink_column(sink_ref, h, rows)
        m = jnp.maximum(jnp.max(s, axis=-1, keepdims=True), sk)
        p = jnp.exp(s - m)
        den = jnp.sum(p, axis=-1, keepdims=True) + jnp.exp(sk - m)
        o = jnp.dot(p.astype(bf16), vh, preferred_element_type=f32) / den
        for g in range(Q_PER_KV):
            outs[h * Q_PER_KV + g] = o[g * rows:(g + 1) * rows]
    o_ref[...] = jnp.concatenate(outs, axis=1).astype(o_ref.dtype)


def _context_attention(lay, qkv, sink):
    sp = lay.sp
    kcol = Q_DIM // KV_DIM
    return pl.pallas_call(
        _ctx_attn_body,
        grid=(lay.bp,),
        in_specs=[pl.BlockSpec(memory_space=pltpu.SMEM),
                  pl.BlockSpec((sp, Q_DIM), lambda b: (b, 0)),
                  pl.BlockSpec((sp, KV_DIM), lambda b: (b, kcol)),
                  pl.BlockSpec((sp, KV_DIM), lambda b: (b, kcol + 1))],
        out_specs=pl.BlockSpec((sp, Q_DIM), lambda b: (b, 0)),
        out_shape=jax.ShapeDtypeStruct((lay.n_prompt, Q_DIM), bf16),
        compiler_params=_params(1),
        name="ctx_attn",
    )(sink, qkv, qkv, qkv)


def _rope_tables(t_len):
    t = jnp.arange(t_len)
    row = (t // GRID_W).astype(f32)
    col = (t % GRID_W).astype(f32)
    quarter = HEAD_DIM // 4
    d = jnp.arange(LANES) % HEAD_DIM
    inv = ROPE_BASE ** (-(d % quarter).astype(f32) / quarter)
    pos = jnp.where((d < HEAD_DIM // 2)[None, :], row[:, None], col[:, None])
    ang = pos * inv[None, :]
    cs, sn = jnp.cos(ang), jnp.sin(ang)
    first = ((d % (2 * quarter)) < quarter)[None, :]
    return cs, jnp.where(first, -sn, 0.0), jnp.where(first, 0.0, sn)


def _rope(x, cs, sa, sb):
    quarter = HEAD_DIM // 4
    outs = []
    for c in range(x.shape[1] // LANES):
        xc = x[:, c * LANES:(c + 1) * LANES]
        outs.append(xc * cs + pltpu.roll(xc, LANES - quarter, axis=1) * sa + pltpu.roll(xc, quarter, axis=1) * sb)
    return outs[0] if len(outs) == 1 else jnp.concatenate(outs, axis=1)


def _lat_attn_body(sink_ref, q_ref, k_ref, v_ref, kc_ref, vc_ref, cs_ref, sa_ref, sb_ref, o_ref):
    j = pl.program_id(1)
    t_len = k_ref.shape[0]
    win = 3 * ATT_Q
    start = pl.multiple_of(jnp.clip((j - 1) * ATT_Q, 0, t_len - win), ATT_Q)
    q0 = pl.multiple_of(j * ATT_Q, ATT_Q)
    q = _rope(q_ref[...], cs_ref[pl.ds(q0, ATT_Q), :], sa_ref[pl.ds(q0, ATT_Q), :],
              sb_ref[pl.ds(q0, ATT_Q), :]) * (HEAD_DIM ** -0.5)
    kw = _rope(k_ref[pl.ds(start, win), :], cs_ref[pl.ds(start, win), :], sa_ref[pl.ds(start, win), :],
               sb_ref[pl.ds(start, win), :])
    vw = v_ref[pl.ds(start, win), :]
    kc = kc_ref[...]
    vc = vc_ref[...]
    rows = Q_PER_KV * ATT_Q
    qpos = q0 + lax.broadcasted_iota(jnp.int32, (rows, win), 0) % ATT_Q
    kpos = start + lax.broadcasted_iota(jnp.int32, (rows, win), 1)
    valid = jnp.abs(qpos - kpos) <= WINDOW
    outs = [None] * N_Q_HEADS
    for h in range(N_KV_HEADS):
        sl = slice(h * HEAD_DIM, (h + 1) * HEAD_DIM)
        qh = _stack_heads(q, h).astype(bf16)
        s_c = _nt(qh, kc[:, sl].astype(bf16))
        s_l = jnp.where(valid, _nt(qh, kw[:, sl].astype(bf16)), NEG)
        sk = _sink_column(sink_ref, h, ATT_Q)
        m = jnp.maximum(jnp.maximum(jnp.max(s_c, axis=-1, keepdims=True),
                                    jnp.max(s_l, axis=-1, keepdims=True)), sk)
        p_c = jnp.exp(s_c - m)
        p_l = jnp.exp(s_l - m)
        den = jnp.sum(p_c, axis=-1, keepdims=True) + jnp.sum(p_l, axis=-1, keepdims=True) + jnp.exp(sk - m)
        o = (jnp.dot(p_c.astype(bf16), vc[:, sl].astype(bf16), preferred_element_type=f32)
             + jnp.dot(p_l.astype(bf16), vw[:, sl].astype(bf16), preferred_element_type=f32)) / den
        for g in range(Q_PER_KV):
            outs[h * Q_PER_KV + g] = o[g * ATT_Q:(g + 1) * ATT_Q]
    o_ref[...] = jnp.concatenate(outs, axis=1).astype(o_ref.dtype)


def _latent_attention(lay, qkv, kc, vc, sink):
    ss = lay.ss
    past = kc.shape[1]
    nq = ss // ATT_Q
    q_base = lay.n_prompt // ATT_Q
    s_base = lay.n_prompt // ss
    assert lay.n_prompt % ss == 0
    kcol = Q_DIM // KV_DIM
    cs, sa, sb = _rope_tables(ss)
    return pl.pallas_call(
        _lat_attn_body,
        grid=(lay.bs, nq),
        in_specs=[pl.BlockSpec(memory_space=pltpu.SMEM),
                  pl.BlockSpec((ATT_Q, Q_DIM), lambda b, j: (q_base + b * nq + j, 0)),
                  pl.BlockSpec((ss, KV_DIM), lambda b, j: (s_base + b, kcol)),
                  pl.BlockSpec((ss, KV_DIM), lambda b, j: (s_base + b, kcol + 1)),
                  pl.BlockSpec((None, past, KV_DIM), lambda b, j: (b, 0, 0)),
                  pl.BlockSpec((None, past, KV_DIM), lambda b, j: (b, 0, 0)),
                  _const_spec((ss, LANES)), _const_spec((ss, LANES)), _const_spec((ss, LANES))],
        out_specs=pl.BlockSpec((ATT_Q, Q_DIM), lambda b, j: (b * nq + j, 0)),
        out_shape=jax.ShapeDtypeStruct((lay.bs * ss, Q_DIM), bf16),
        compiler_params=_params(2),
        name="lat_attn",
    )(sink, qkv, qkv, qkv, kc, vc, cs, sa, sb)


CONV_HALO = 16
CONV_SUB = 32


def _dwconv_body(lay, u_ref, prev_ref, next_ref, w_ref, b_ref, g_ref, be_ref, o_ref, buf_ref):
    i = pl.program_id(0)
    pos, cnt = lay.seq_pos(i)
    buf_ref[pl.ds(0, CONV_HALO), :] = jnp.where(pos > 0, prev_ref[...], 0.0)
    buf_ref[pl.ds(CONV_HALO, ROW_TILE), :] = u_ref[...]
    buf_ref[pl.ds(CONV_HALO + ROW_TILE, CONV_HALO), :] = jnp.where(pos < cnt - 1, next_ref[...], 0.0)
    w = w_ref[...]
    off = CONV_HALO - CONV_PAD

    for s in range(ROW_TILE // CONV_SUB):
        r0 = s * CONV_SUB
        acc = jnp.zeros((CONV_SUB, D), f32)
        for t in range(CONV_WIDTH):
            acc = acc + buf_ref[pl.ds(r0 + off + t, CONV_SUB), :] * w[t:t + 1, :]
        z = _layer_norm(acc + b_ref[...], g_ref[...], be_ref[...])
        o_ref[pl.ds(r0, CONV_SUB), :] = _silu(z).astype(o_ref.dtype)


def _dwconv_ln_swish(lay, u, w_dw, b_dw, ln_g, ln_b):
    hpt = ROW_TILE // CONV_HALO
    last = lay.n // CONV_HALO - 1
    return pl.pallas_call(
        functools.partial(_dwconv_body, lay),
        grid=(lay.n_tiles,),
        in_specs=[lay.row_spec(),
                  pl.BlockSpec((CONV_HALO, D), lambda i: (jnp.maximum(i * hpt - 1, 0), 0)),
                  pl.BlockSpec((CONV_HALO, D), lambda i: (jnp.minimum((i + 1) * hpt, last), 0)),
                  _const_spec((CONV_WIDTH, D)), _const_spec((1, D)), _const_spec((1, D)), _const_spec((1, D))],
        out_specs=lay.row_spec(),
        out_shape=jax.ShapeDtypeStruct((lay.n, D), bf16),
        scratch_shapes=[pltpu.VMEM((ROW_TILE + 2 * CONV_HALO, D), f32)],
        compiler_params=_params(1),
        name="dwconv",
    )(u, u, u, w_dw, b_dw.reshape(1, D), ln_g.reshape(1, D), ln_b.reshape(1, D))


def _conformer(lay, x, m, w_pw1, b_pw1, w_dw, b_dw, ln_g, ln_b, w_pw2, b_pw2):
    u = _mod_proj(lay, x, m[0], m[1], w_pw1, b_pw1, glu=True, name="conv_pw1_glu")
    return _dwconv_ln_swish(lay, u, w_dw, b_dw, ln_g, ln_b)


LORA_PAD = 128
GATE_PAD = 256


def _bdot(a, w):
    return jnp.dot(a.astype(bf16), w, preferred_element_type=f32)


def _rwkv_proj_body(lay, x_ref, prev_ref, next_ref, sh_ref, sc_ref, mu_ref, wrkv_ref, w0_ref, w1_ref, w2_ref,
                    a0_ref, a1_ref, a2_ref, g1_ref, g2_ref, r_ref, k_ref, v_ref, g_ref, lw_ref, a_ref):
    i = pl.program_id(0)
    pos, cnt = lay.seq_pos(i)
    sc = 1.0 + sc_ref[...]
    sh = sh_ref[...]
    h = x_ref[...] * sc + sh
    hb = prev_ref.shape[0]
    h_prev = jnp.where(pos > 0, prev_ref[hb - 1:hb, :] * sc + sh, 0.0)
    h_next = jnp.where(pos < cnt - 1, next_ref[0:1, :] * sc + sh, 0.0)
    row = lax.broadcasted_iota(jnp.int32, (ROW_TILE, 1), 0)
    below = jnp.where(row == 0, h_prev, pltpu.roll(h, 1, axis=0))
    above = jnp.where(row == ROW_TILE - 1, h_next, pltpu.roll(h, ROW_TILE - 1, axis=0))
    xx = 0.5 * (below + above) - h

    def mix(j):
        return (h + xx * mu_ref[j:j + 1, :]).astype(bf16)

    r_ref[...] = _bdot(mix(0), wrkv_ref[0])
    k_ref[...] = _bdot(mix(2), wrkv_ref[1])
    v_ref[...] = _bdot(mix(3), wrkv_ref[2])
    g_ref[...] = _bdot(jax.nn.sigmoid(_bdot(mix(5), g1_ref[...])), g2_ref[...])
    xw = mix(1)
    xa = mix(4)
    for d in range(2):
        z = w0_ref[d:d + 1, :] + _bdot(jnp.tanh(_bdot(xw, w1_ref[d])), w2_ref[d])
        softplus = jnp.maximum(-z, 0.0) + jnp.log(1.0 + jnp.exp(-jnp.abs(z)))
        lw_ref[d] = -jnp.exp(-softplus - 0.5)
        a_ref[d] = jax.nn.sigmoid(a0_ref[d:d + 1, :] + _bdot(_bdot(xa, a1_ref[d]), a2_ref[d]))


def _pad_to(w, axis, size):
    pad = [(0, 0)] * w.ndim
    pad[axis] = (0, size - w.shape[axis])
    return jnp.pad(w, pad)


def _rwkv_project(lay, x, m, mu, w_rkv, w0, w1, w2, a0, a1, a2, g1, g2):
    halo = 8
    hpt = ROW_TILE // halo
    last = lay.n // halo - 1
    nd = (lay.n, D)
    consts = [mu, w_rkv.astype(bf16), w0,
              _pad_to(w1, 2, LORA_PAD).astype(bf16), _pad_to(w2, 1, LORA_PAD).astype(bf16), a0,
              _pad_to(a1, 2, LORA_PAD).astype(bf16), _pad_to(a2, 1, LORA_PAD).astype(bf16),
              _pad_to(g1, 1, GATE_PAD).astype(bf16), _pad_to(g2, 0, GATE_PAD).astype(bf16)]
    return pl.pallas_call(
        functools.partial(_rwkv_proj_body, lay),
        grid=(lay.n_tiles,),
        in_specs=[lay.row_spec(),
                  pl.BlockSpec((halo, D), lambda i: (jnp.maximum(i * hpt - 1, 0), 0)),
                  pl.BlockSpec((halo, D), lambda i: (jnp.minimum((i + 1) * hpt, last), 0)),
                  lay.mod_spec(), lay.mod_spec()] + [_const_spec(c.shape) for c in consts],
        out_specs=[lay.row_spec()] * 4 + [pl.BlockSpec((2, ROW_TILE, D), lambda i: (0, i, 0))] * 2,
        out_shape=[jax.ShapeDtypeStruct(nd, f32)] * 4 + [jax.ShapeDtypeStruct((2,) + nd, f32)] * 2,
        compiler_params=_params(1),
        name="rwkv_proj",
    )(x, x, x, m[0], m[1], *consts)


def _split_dot(x, c, parts):
    acc = None
    for _ in range(parts):
        hi = x.astype(bf16)
        t = jnp.dot(hi, c, preferred_element_type=f32)
        acc = t if acc is None else acc + t
        x = x - hi.astype(f32)
    return acc


def _split_dot_left(c, x, parts):
    acc = None
    for _ in range(parts):
        hi = x.astype(bf16)
        t = jnp.dot(c, hi, preferred_element_type=f32)
        acc = t if acc is None else acc + t
        x = x - hi.astype(f32)
    return acc


def _mm(a, b):
    return jnp.dot(a.astype(bf16), b.astype(bf16), preferred_element_type=f32)


def _split3(x):
    x1 = x.astype(bf16)
    r1 = x - x1.astype(f32)
    x2 = r1.astype(bf16)
    x3 = (r1 - x2.astype(f32)).astype(bf16)
    return x1, x2, x3


def _mm_f32(a3, b):
    b1, b2, b3 = _split3(b)
    n = a3[0].shape[0]
    r1 = jnp.dot(jnp.concatenate(a3, axis=0), b1, preferred_element_type=f32)
    r2 = jnp.dot(jnp.concatenate(a3[:2], axis=0), b2, preferred_element_type=f32)
    r3 = jnp.dot(a3[0], b3, preferred_element_type=f32)
    small = r1[2 * n:] + r2[n:] + r3
    return r1[:n] + ((r1[n:2 * n] + r2[:n]) + small)


def _mm_nt(a, b):
    return lax.dot_general(a.astype(bf16), b.astype(bf16), (((1,), (1,)), ((), ())), preferred_element_type=f32)


def _scan_body(lay, rev, r_ref, k_ref, v_ref, lw_ref, a_ref, kk_ref, ka_ref, rk_ref, s0_ref, tri_ref, bd_ref,
               y_ref, bon_ref, sfin_ref, state_ref):
    t_idx = pl.program_id(1)
    i = (lay.n_tiles - 1 - t_idx) if rev else t_idx
    pos, cnt = lay.seq_pos(i)
    first = (pos == cnt - 1) if rev else (pos == 0)
    final = (pos == 0) if rev else (pos == cnt - 1)
    c_len = SCAN_CHUNK
    n_chunks = SCAN_ROWS // c_len

    @pl.when(first)
    def _():
        state_ref[...] = s0_ref[...]

    lane = lax.broadcasted_iota(jnp.int32, (1, LANES), 1)
    head0 = lane < RWKV_HEAD
    rr = lax.broadcasted_iota(jnp.int32, (2 * c_len, 2 * c_len), 0)
    cc = lax.broadcasted_iota(jnp.int32, (2 * c_len, 2 * c_len), 1)
    same = (rr // c_len) == (cc // c_len)
    tt = rr % c_len
    ss = cc % c_len
    strict = same & ((tt < ss) if rev else (tt > ss))
    incl = same & ((tt <= ss) if rev else (tt >= ss))
    tri = tri_ref[...]
    bd = bd_ref[...]

    def stack(xc):
        return jnp.concatenate([jnp.where(head0, xc, 0.0), jnp.where(head0, 0.0, xc)], axis=0)

    def pair_chunk(rows, p):
        cols = slice(p * LANES, (p + 1) * LANES)
        r = r_ref[rows, cols]
        k = k_ref[rows, cols]
        v = v_ref[rows, cols]
        lw = lw_ref[rows, cols]
        a = a_ref[rows, cols]
        kk = k * kk_ref[:, cols]
        nrm = jnp.sqrt(_split_dot(kk * kk, bd, 2))
        kk = kk / jnp.maximum(nrm, 1e-12)
        b = kk * a
        kd = k * (1.0 + (a - 1.0) * ka_ref[:, cols])
        bon_ref[rows, cols] = _split_dot(r * kd * rk_ref[:, cols], bd, 2) * v
        cum = _split_dot_left(tri, lw, 3)
        tot = cum[0:1, :] if rev else cum[c_len - 1:c_len, :]
        e_tot = jnp.exp(tot)
        e_neg = jnp.exp(-cum)
        a_t = -kk * jnp.exp(cum - lw)
        r_t = r * jnp.exp(cum)
        b_t = b * e_neg
        k_t = kd * e_neg
        ar = jnp.concatenate([stack(a_t), stack(r_t)], axis=0)
        bk = jnp.concatenate([stack(b_t), stack(k_t)], axis=0)
        m = _mm_nt(ar, bk)
        h2 = 2 * c_len
        n_ab = jnp.where(strict, m[:h2, :h2], 0.0)
        a_ak = jnp.where(strict, m[:h2, h2:], 0.0)
        a_rb = jnp.where(incl, m[h2:, :h2], 0.0)
        a_rk = jnp.where(incl, m[h2:, h2:], 0.0)
        state = state_ref[p]
        x0 = _mm_nt(ar, state)
        v_st = stack(v)
        sa = x0[:h2] + _mm(a_ak, v_st)
        pw = n_ab
        span = 1
        while span < c_len:
            pw3 = _split3(pw)
            sa = sa + _mm_f32(pw3, sa)
            span *= 2
            if span < c_len:
                pw = _mm_f32(pw3, pw)
        sav = jnp.concatenate([sa, v_st], axis=0)
        y_st = x0[h2:] + _mm(jnp.concatenate([a_rb, a_rk], axis=1), sav)
        y_ref[rows, cols] = y_st[:c_len] + y_st[c_len:]
        state_ref[p] = state * e_tot + _mm(sav.T, bk * e_tot)

    def chunk(c, carry):
        ce = (n_chunks - 1 - c) if rev else c
        rows = pl.ds(pl.multiple_of(ce * c_len, c_len), c_len)
        for p in range(SCAN_PAIRS):
            pair_chunk(rows, p)
        return carry

    lax.fori_loop(0, n_chunks, chunk, 0)

    @pl.when(final)
    def _():
        sfin_ref[...] = state_ref[...]


def _block_diag_states(s):
    b = s.shape[0]
    s = s.reshape(b, RWKV_HEADS // 2, 2, RWKV_HEAD, RWKV_HEAD)
    eye = jnp.eye(2, dtype=s.dtype)
    out = s[:, :, :, :, None, :] * eye[None, None, :, None, :, None]
    return out.reshape(b, RWKV_HEADS // 2, LANES, LANES)


def _diag_blocks(w):
    b = w.shape[0]
    w = w.reshape(b, RWKV_HEADS // 2, 2, RWKV_HEAD, 2, RWKV_HEAD)
    return jnp.stack([w[:, :, 0, :, 0, :], w[:, :, 1, :, 1, :]], axis=2).reshape(b, RWKV_HEADS, RWKV_HEAD, RWKV_HEAD)


def _rwkv_scan(lay, rev, r, k, v, lw, a, k_k, k_a, r_k, s0):
    d = 1 if rev else 0
    n_seq = lay.bp + lay.bs
    c_len = SCAN_CHUNK
    idx = jnp.arange(c_len)
    tri = ((idx[:, None] <= idx[None, :]) if rev else (idx[:, None] >= idx[None, :])).astype(bf16)
    hh = jnp.arange(LANES) // RWKV_HEAD
    bd = (hh[:, None] == hh[None, :]).astype(bf16)

    def tile(t):
        return (lay.n_tiles - 1 - t) if rev else t

    def seq(t):
        i = tile(t)
        return jnp.where(i < lay.prompt_tiles, i // lay.tiles_per_pseq,
                         lay.bp + (i - lay.prompt_tiles) // lay.tiles_per_sseq)

    width = SCAN_PAIRS * LANES
    row = pl.BlockSpec((SCAN_ROWS, width), lambda p, t: (tile(t), p))
    row_d = pl.BlockSpec((None, SCAN_ROWS, width), lambda p, t: (d, tile(t), p))
    par = pl.BlockSpec((1, width), lambda p, t: (0, p))
    st = pl.BlockSpec((None, SCAN_PAIRS, LANES, LANES), lambda p, t: (seq(t), p, 0, 0))
    return pl.pallas_call(
        functools.partial(_scan_body, lay, rev),
        grid=(RWKV_HEADS // 2 // SCAN_PAIRS, lay.n_tiles),
        in_specs=[row, row, row, row_d, row_d, par, par, par, st,
                  _const_spec((c_len, c_len)), _const_spec((LANES, LANES))],
        out_specs=[row, row, st],
        out_shape=[jax.ShapeDtypeStruct((lay.n, D), f32), jax.ShapeDtypeStruct((lay.n, D), f32),
                   jax.ShapeDtypeStruct((n_seq, RWKV_HEADS // 2, LANES, LANES), f32)],
        scratch_shapes=[pltpu.VMEM((SCAN_PAIRS, LANES, LANES), f32)],
        compiler_params=_params(2),
        name="rwkv_scan_bwd" if rev else "rwkv_scan_fwd",
    )(r, k, v, lw, a, k_k.reshape(1, D), k_a.reshape(1, D), r_k.reshape(1, D), s0, tri, bd)


def _rwkv_out_body(yf_ref, yb_ref, bf_ref, bb_ref, g_ref, gg_ref, gb_ref, bd_ref, o_ref):
    y = yf_ref[...] + yb_ref[...]
    bd = bd_ref[...]
    inv = 1.0 / RWKV_HEAD
    outs = []
    for c in range(D // LANES):
        yc = y[:, c * LANES:(c + 1) * LANES]
        cen = yc - _split_dot(yc, bd, 2) * inv
        var = _split_dot(cen * cen, bd, 2) * inv
        outs.append(cen * lax.rsqrt(var + GN_EPS))
    yn = jnp.concatenate(outs, axis=1) * gg_ref[...] + gb_ref[...]
    o_ref[...] = ((yn + bf_ref[...] + bb_ref[...]) * g_ref[...]).astype(o_ref.dtype)


def _rwkv(lay, x, m, s0_f, s0_b, mu, w_rkv, w0, w1, w2, a0, a1, a2, g1, g2, k_k, k_a, r_k, gn_g, gn_b):
    r, k, v, g, lw, a = _rwkv_project(lay, x, m, mu, w_rkv, w0, w1, w2, a0, a1, a2, g1, g2)
    zeros = jnp.zeros((lay.bp, RWKV_HEADS // 2, LANES, LANES), f32)
    outs = []
    for rev, s0 in ((False, s0_f), (True, s0_b)):
        s_all = jnp.concatenate([zeros, _block_diag_states(s0.astype(f32))], axis=0)
        outs.append(_rwkv_scan(lay, rev, r, k, v, lw, a, k_k, k_a, r_k, s_all))
    hh = jnp.arange(LANES) // RWKV_HEAD
    bd = (hh[:, None] == hh[None, :]).astype(bf16)
    y = pl.pallas_call(
        _rwkv_out_body,
        grid=(lay.n_tiles,),
        in_specs=[lay.row_spec()] * 5 + [_const_spec((1, D)), _const_spec((1, D)), _const_spec((LANES, LANES))],
        out_specs=lay.row_spec(),
        out_shape=jax.ShapeDtypeStruct((lay.n, D), bf16),
        compiler_params=_params(1),
        name="rwkv_groupnorm_gate",
    )(outs[0][0], outs[1][0], outs[0][1], outs[1][1], g, gn_g.reshape(1, D), gn_b.reshape(1, D), bd)
    return y, _diag_blocks(outs[0][2][:lay.bp]), _diag_blocks(outs[1][2][:lay.bp])


def _router_body(x_ref, sh_ref, sc_ref, w_ref, wlo_ref, b_ref, h_ref, route_ref):
    h = x_ref[...] * (1.0 + sc_ref[...]) + sh_ref[...]
    h_hi = h.astype(bf16)
    h_ref[...] = h_hi
    h_lo = (h - h_hi.astype(f32)).astype(bf16)
    lg = (jnp.dot(h_hi, w_ref[...], preferred_element_type=f32)
          + (jnp.dot(h_hi, wlo_ref[...], preferred_element_type=f32)
             + jnp.dot(h_lo, w_ref[...], preferred_element_type=f32))) + b_ref[...]
    lane = lax.broadcasted_iota(jnp.int32, lg.shape, 1)

    def top1(vals):
        mx = jnp.max(vals, axis=-1, keepdims=True)
        idx = jnp.min(jnp.where(vals == mx, lane, LANES), axis=-1, keepdims=True)
        return mx, idx

    gl = jnp.where(lane < N_GROUPS, lg, NEG)
    g_max, g_idx = top1(gl)
    p_g = 1.0 / jnp.sum(jnp.exp(gl - g_max), axis=-1, keepdims=True)
    lo = N_GROUPS + g_idx * EXPERTS_PER_GROUP
    el = jnp.where((lane >= lo) & (lane < lo + EXPERTS_PER_GROUP), lg, NEG)
    m1, i1 = top1(el)
    m2, i2 = top1(jnp.where(lane == i1, NEG, el))
    e2 = jnp.exp(m2 - m1)
    w1 = p_g / (1.0 + e2)
    w2 = p_g * e2 / (1.0 + e2)
    route = jnp.where(lane == 0, (i1 - N_GROUPS).astype(f32),
                      jnp.where(lane == 1, (i2 - N_GROUPS).astype(f32),
                                jnp.where(lane == 2, w1, jnp.where(lane == 3, w2, 0.0))))
    route_ref[...] = route


def _expert_body(be_ref, nu_ref, x_ref, wgu_ref, wd_ref, o_ref, wgu_s, wd_s):
    i = pl.program_id(0)
    used = i < nu_ref[0]

    @pl.when(used & ((i == 0) | (be_ref[i] != be_ref[jnp.maximum(i - 1, 0)])))
    def _():
        wgu_s[...] = wgu_ref[...].astype(bf16)
        wd_s[...] = wd_ref[...].astype(bf16)

    @pl.when(used)
    def _():
        u = jnp.dot(x_ref[...], wgu_s[...], preferred_element_type=f32)
        act = _silu(u[:, :EXPERT_FF]) * u[:, EXPERT_FF:]
        o_ref[...] = jnp.dot(act.astype(bf16), wd_s[...], preferred_element_type=f32).astype(o_ref.dtype)

    @pl.when(jnp.logical_not(used))
    def _():
        o_ref[...] = jnp.zeros_like(o_ref)


def _moe_out_body(y_ref0, y_ref1, route_ref, x_ref, gate_ref, g_ref, be_ref, o_ref):
    route = route_ref[...]
    y = route[:, 2:3] * y_ref0[...].astype(f32) + route[:, 3:4] * y_ref1[...].astype(f32)
    z = ALPHA * x_ref[...] + gate_ref[...] * y
    o_ref[...] = _layer_norm(z, g_ref[...], be_ref[...])


def _moe_layer(lay, x, m, w_group, b_group, w_expert, b_expert, w_gu, w_down, ln_g, ln_b, layer=None):
    if layer is None:
        w_gu, w_down, layer = w_gu[None], w_down[None], 0
    n = lay.n
    n_route = N_GROUPS + N_EXPERTS
    w_r = _pad_to(jnp.concatenate([w_group, w_expert], axis=1).astype(f32), 1, LANES)
    b_r = _pad_to(jnp.concatenate([b_group, b_expert]).astype(f32).reshape(1, n_route), 1, LANES)
    w_hi = w_r.astype(bf16)
    h, route = pl.pallas_call(
        _router_body,
        grid=(lay.n_tiles,),
        in_specs=[lay.row_spec(), lay.mod_spec(), lay.mod_spec(), _const_spec((D, LANES)), _const_spec((D, LANES)),
                  _const_spec((1, LANES))],
        out_specs=[lay.row_spec(), lay.row_spec(LANES)],
        out_shape=[jax.ShapeDtypeStruct((n, D), bf16), jax.ShapeDtypeStruct((n, LANES), f32)],
        compiler_params=_params(1),
        name="moe_router",
    )(x, m[3], m[4], w_hi, (w_r - w_hi.astype(f32)).astype(bf16), b_r)

    s = n * TOP_K
    e_flat = route[:, :TOP_K].astype(jnp.int32).reshape(-1)
    pair = jnp.arange(s, dtype=jnp.int32)
    e_sorted, order = lax.sort((e_flat, pair), num_keys=1, is_stable=True)
    experts = jnp.arange(N_EXPERTS, dtype=jnp.int32)
    counts = jnp.sum((e_flat[:, None] == experts[None, :]).astype(jnp.int32), axis=0)
    padded = (counts + MOE_ROWS - 1) // MOE_ROWS * MOE_ROWS
    pad_end = jnp.cumsum(padded)
    pad_start = pad_end - padded
    start = jnp.cumsum(counts) - counts
    dest_sorted = pad_start[e_sorted] + pair - start[e_sorted]
    n_blocks = -(-s // MOE_ROWS) + N_EXPERTS
    p_rows = n_blocks * MOE_ROWS
    block_start = jnp.arange(n_blocks, dtype=jnp.int32) * MOE_ROWS
    block_expert = jnp.minimum(jnp.sum((pad_end[None, :] <= block_start[:, None]).astype(jnp.int32), axis=1),
                               N_EXPERTS - 1)
    n_used = (pad_end[N_EXPERTS - 1] // MOE_ROWS).astype(jnp.int32).reshape(1)
    slot = jnp.arange(p_rows, dtype=jnp.int32)
    slot_expert = jnp.repeat(block_expert, MOE_ROWS)
    in_run = jnp.minimum(slot - pad_start[slot_expert], jnp.maximum(counts[slot_expert] - 1, 0))
    slot_token = order[jnp.clip(start[slot_expert] + in_run, 0, s - 1)] // TOP_K
    _, dest = lax.sort((order, dest_sorted), num_keys=1)
    dest = dest.reshape(n, TOP_K)

    xb = h.at[slot_token].get(mode="promise_in_bounds")
    yb = pl.pallas_call(
        _expert_body,
        grid_spec=pltpu.PrefetchScalarGridSpec(
            num_scalar_prefetch=2,
            grid=(n_blocks,),
            in_specs=[pl.BlockSpec((MOE_ROWS, D), lambda i, be, nu: (jnp.minimum(i, nu[0] - 1), 0)),
                      pl.BlockSpec((None, None, D, 2 * EXPERT_FF), lambda i, be, nu: (layer, be[i], 0, 0)),
                      pl.BlockSpec((None, None, EXPERT_FF, D), lambda i, be, nu: (layer, be[i], 0, 0))],
            out_specs=pl.BlockSpec((MOE_ROWS, D), lambda i, be, nu: (i, 0)),
            scratch_shapes=[pltpu.VMEM((D, 2 * EXPERT_FF), bf16), pltpu.VMEM((EXPERT_FF, D), bf16)]),
        out_shape=jax.ShapeDtypeStruct((p_rows, D), bf16),
        compiler_params=_params(1),
        name="moe_experts",
    )(block_expert, n_used, xb, w_gu, w_down)
    y0 = yb.at[dest[:, 0]].get(mode="promise_in_bounds")
    y1 = yb.at[dest[:, 1]].get(mode="promise_in_bounds")

    return pl.pallas_call(
        _moe_out_body,
        grid=(lay.n_tiles,),
        in_specs=[lay.row_spec(), lay.row_spec(), lay.row_spec(LANES), lay.row_spec(), lay.mod_spec(),
                  _const_spec((1, D)), _const_spec((1, D))],
        out_specs=lay.row_spec(),
        out_shape=jax.ShapeDtypeStruct((n, D), f32),
        compiler_params=_params(1),
        name="moe_combine_ln",
    )(y0, y1, route, x, m[5], ln_g.reshape(1, D), ln_b.reshape(1, D))


def kernel(x_prompt, x_sample, c, c_ctx, cache_attn_k, cache_attn_v, state_rwkv_fwd, state_rwkv_bwd, mod_w, mod_b, post_ln_g, post_ln_b, attn_w_qkv, attn_w_o, attn_sink, conv_w_pw1, conv_b_pw1, conv_w_dw, conv_b_dw, conv_ln_g, conv_ln_b, conv_w_pw2, conv_b_pw2, rwkv_mu, rwkv_w_rkv, rwkv_w0, rwkv_w1, rwkv_w2, rwkv_a0, rwkv_a1, rwkv_a2, rwkv_g1, rwkv_g2, rwkv_k_k, rwkv_k_a, rwkv_r_k, rwkv_gn_g, rwkv_gn_b, rwkv_w_o, moe_w_group, moe_b_group, moe_w_expert, moe_b_expert, moe_w_gate_up, moe_w_down):
    bp, sp, _ = x_prompt.shape
    bs, ss, _ = x_sample.shape
    assert 1 + bs <= MOD_ROWS
    lay = _Layout(bp, sp, bs, ss)
    x = jnp.concatenate([x_prompt.reshape(bp * sp, D), x_sample.reshape(bs * ss, D)], axis=0)
    cond = jnp.concatenate([c_ctx[None, :], c, jnp.zeros((MOD_ROWS - 1 - bs, D), f32)], axis=0)
    modt = _modulation_table(cond, mod_w, mod_b)
    zero_b = jnp.zeros((D,), f32)
    new_k, new_v, new_sf, new_sb = [], [], [], []
    n_attn = n_conv = n_rwkv = 0
    for i in range(DEPTH):
        m = modt[i]
        kind = i % 3
        if kind == 0:
            j = n_attn
            qkv = _mod_proj(lay, x, m[0], m[1], attn_w_qkv[j], jnp.zeros((Q_DIM + 2 * KV_DIM,), f32), name="attn_qkv")
            past = cache_attn_k.shape[2]
            o_p = _context_attention(lay, qkv, attn_sink[j])
            o_s = _latent_attention(lay, qkv, cache_attn_k[:, j].reshape(bs, past, KV_DIM),
                                    cache_attn_v[:, j].reshape(bs, past, KV_DIM), attn_sink[j])
            y = jnp.concatenate([o_p, o_s], axis=0)
            w_out, b_out = attn_w_o[j], zero_b
            new_k.append(qkv[:lay.n_prompt, Q_DIM:Q_DIM + KV_DIM].reshape(bp, sp, N_KV_HEADS, HEAD_DIM))
            new_v.append(qkv[:lay.n_prompt, Q_DIM + KV_DIM:].reshape(bp, sp, N_KV_HEADS, HEAD_DIM))
            n_attn += 1
        elif kind == 1:
            j = n_conv
            y = _conformer(lay, x, m, conv_w_pw1[j], conv_b_pw1[j], conv_w_dw[j], conv_b_dw[j], conv_ln_g[j],
                           conv_ln_b[j], conv_w_pw2[j], conv_b_pw2[j])
            w_out, b_out = conv_w_pw2[j], conv_b_pw2[j]
            n_conv += 1
        else:
            j = n_rwkv
            y, sf, sb = _rwkv(lay, x, m, state_rwkv_fwd[:, j], state_rwkv_bwd[:, j], rwkv_mu[j], rwkv_w_rkv[j],
                              rwkv_w0[j], rwkv_w1[j], rwkv_w2[j], rwkv_a0[j], rwkv_a1[j], rwkv_a2[j], rwkv_g1[j],
                              rwkv_g2[j], rwkv_k_k[j], rwkv_k_a[j], rwkv_r_k[j], rwkv_gn_g[j], rwkv_gn_b[j])
            w_out, b_out = rwkv_w_o[j], zero_b
            new_sf.append(sf)
            new_sb.append(sb)
            n_rwkv += 1
        x = _out_proj_ln(lay, y, w_out, b_out, x, m[2], post_ln_g[i, 0], post_ln_b[i, 0], name="mixer_out_ln")
        x = _moe_layer(lay, x, m, moe_w_group[i], moe_b_group[i], moe_w_expert[i], moe_b_expert[i],
                       moe_w_gate_up, moe_w_down, post_ln_g[i, 1], post_ln_b[i, 1], layer=i)
    y_prompt = x[:lay.n_prompt].reshape(bp, sp, D)
    y_sample = x[lay.n_prompt:].reshape(bs, ss, D)
    return (y_prompt, y_sample, jnp.stack(new_k, axis=1), jnp.stack(new_v, axis=1),
            jnp.stack(new_sf, axis=1).astype(x_prompt.dtype), jnp.stack(new_sb, axis=1).astype(x_prompt.dtype))
```

```python
import functools

import jax
import jax.numpy as jnp
from jax import lax
from jax.experimental import pallas as pl
from jax.experimental.pallas import tpu as pltpu

f32 = jnp.float32
bf16 = jnp.bfloat16

D = 1024
DEPTH = 4
GRID_W = 64
HEAD_DIM = 64
N_Q_HEADS = D // HEAD_DIM
N_KV_HEADS = N_Q_HEADS // 4
Q_PER_KV = N_Q_HEADS // N_KV_HEADS
Q_DIM = N_Q_HEADS * HEAD_DIM
KV_DIM = N_KV_HEADS * HEAD_DIM
WINDOW = 128
ROPE_BASE = 10000.0
CONV_WIDTH = 31
CONV_PAD = (CONV_WIDTH - 1) // 2
RWKV_HEAD = 64
RWKV_HEADS = D // RWKV_HEAD
GN_EPS = 64e-5
N_GROUPS = 4
EXPERTS_PER_GROUP = 8
N_EXPERTS = N_GROUPS * EXPERTS_PER_GROUP
TOP_K = 2
EXPERT_FF = D // 2
LN_EPS = 1e-5
ALPHA = (2 * DEPTH) ** 0.25

LANES = 128
ROW_TILE = 256
WIDE_TILE = 512
MOE_ROWS = 512
ATT_Q = 128
SCAN_CHUNK = 64
SCAN_ROWS = 256
SCAN_PAIRS = 8
MOD_ROWS = 16
VMEM_LIMIT = 56 * 1024 * 1024
NEG = -1e30
HI = lax.Precision.HIGHEST


def _params(n_axes):
    return pltpu.CompilerParams(dimension_semantics=("arbitrary",) * n_axes,
                                vmem_limit_bytes=VMEM_LIMIT)


def _const_spec(shape):
    nd = len(shape)
    return pl.BlockSpec(shape, lambda *_: (0,) * nd)


class _Layout:
    def __init__(self, bp, sp, bs, ss):
        self.bp, self.sp, self.bs, self.ss = bp, sp, bs, ss
        self.n_prompt = bp * sp
        self.n = bp * sp + bs * ss
        self.tile = ROW_TILE
        assert sp % ROW_TILE == 0 and ss % ROW_TILE == 0
        self.n_tiles = self.n // ROW_TILE
        self.prompt_tiles = self.n_prompt // ROW_TILE
        self.tiles_per_sseq = ss // ROW_TILE
        self.tiles_per_pseq = sp // ROW_TILE

    def mod_row(self, i):
        return jnp.where(i < self.prompt_tiles, 0, 1 + (i - self.prompt_tiles) // self.tiles_per_sseq)

    def seq_pos(self, i):
        in_p = i < self.prompt_tiles
        pos = jnp.where(in_p, i % self.tiles_per_pseq, (i - self.prompt_tiles) % self.tiles_per_sseq)
        cnt = jnp.where(in_p, self.tiles_per_pseq, self.tiles_per_sseq)
        return pos, cnt

    def row_spec(self, width=D, col=0):
        return pl.BlockSpec((self.tile, width), lambda i: (i, col))

    def mod_spec(self):
        return pl.BlockSpec((None, 1, D), lambda i: (self.mod_row(i), 0, 0))


class _WideLayout(_Layout):
    def __init__(self, lay):
        self.bp, self.sp, self.bs, self.ss = lay.bp, lay.sp, lay.bs, lay.ss
        self.n_prompt, self.n = lay.n_prompt, lay.n
        self.tile = WIDE_TILE
        assert self.n_prompt % WIDE_TILE == 0 and self.ss % WIDE_TILE == 0
        self.n_tiles = self.n // WIDE_TILE
        self.prompt_tiles = self.n_prompt // WIDE_TILE
        self.tiles_per_sseq = self.ss // WIDE_TILE


def _layer_norm(z, g, b, eps=LN_EPS):
    mu = jnp.mean(z, axis=-1, keepdims=True)
    zc = z - mu
    var = jnp.mean(zc * zc, axis=-1, keepdims=True)
    return zc * lax.rsqrt(var + eps) * g + b


def _silu(x):
    return x * jax.nn.sigmoid(x)


def _mod_body(c_ref, w_ref, b_ref, o_ref):
    o_ref[...] = jnp.dot(_silu(c_ref[...]), w_ref[...], precision=HI,
                         preferred_element_type=f32) + b_ref[...]


def _modulation_table(cond, mod_w, mod_b):
    out = pl.pallas_call(
        _mod_body,
        grid=(DEPTH, 6),
        in_specs=[_const_spec((MOD_ROWS, D)),
                  pl.BlockSpec((None, D, D), lambda l, j: (l, 0, j)),
                  pl.BlockSpec((None, 1, D), lambda l, j: (l, 0, j))],
        out_specs=pl.BlockSpec((None, None, MOD_ROWS, D), lambda l, j: (l, j, 0, 0)),
        out_shape=jax.ShapeDtypeStruct((DEPTH, 6, MOD_ROWS, D), f32),
        compiler_params=_params(2),
        name="modulation",
    )(cond, mod_w, mod_b.reshape(DEPTH, 1, 6 * D))
    return out.reshape(DEPTH, 6, MOD_ROWS, 1, D)


def _proj_body(x_ref, sh_ref, sc_ref, w_ref, b_ref, o_ref, *, glu):
    h = x_ref[...] * (1.0 + sc_ref[...]) + sh_ref[...]
    u = jnp.dot(h.astype(bf16), w_ref[...], preferred_element_type=f32) + b_ref[...]
    if glu:
        half = u.shape[1] // 2
        u = u[:, :half] * jax.nn.sigmoid(u[:, half:])
    o_ref[...] = u.astype(o_ref.dtype)


def _mod_proj(lay, x, shift, scale, w, b, *, glu=False, out_dtype=f32, name="proj"):
    k, n = w.shape
    n_out = n // 2 if glu else n
    return pl.pallas_call(
        functools.partial(_proj_body, glu=glu),
        grid=(lay.n_tiles,),
        in_specs=[lay.row_spec(), lay.mod_spec(), lay.mod_spec(),
                  _const_spec((k, n)), _const_spec((1, n))],
        out_specs=lay.row_spec(n_out),
        out_shape=jax.ShapeDtypeStruct((lay.n, n_out), out_dtype),
        compiler_params=_params(1),
        name=name,
    )(x, shift, scale, w.astype(bf16), b.reshape(1, n))


def _out_ln_body(y_ref, w_ref, b_ref, x_ref, gate_ref, g_ref, be_ref, o_ref):
    t = jnp.dot(y_ref[...].astype(bf16), w_ref[...], preferred_element_type=f32) + b_ref[...]
    z = ALPHA * x_ref[...] + gate_ref[...] * t
    o_ref[...] = _layer_norm(z, g_ref[...], be_ref[...])


def _out_proj_ln(lay, y, w, b, x, gate, ln_g, ln_b, name="out_ln"):
    k = w.shape[0]
    return pl.pallas_call(
        _out_ln_body,
        grid=(lay.n_tiles,),
        in_specs=[lay.row_spec(k), _const_spec((k, D)), _const_spec((1, D)),
                  lay.row_spec(), lay.mod_spec(), _const_spec((1, D)), _const_spec((1, D))],
        out_specs=lay.row_spec(),
        out_shape=jax.ShapeDtypeStruct((lay.n, D), f32),
        compiler_params=_params(1),
        name=name,
    )(y, w.astype(bf16), b.reshape(1, D), x, gate, ln_g.reshape(1, D), ln_b.reshape(1, D))


def _sink_column(sink_ref, h, rows_per_head):
    r = lax.broadcasted_iota(jnp.int32, (Q_PER_KV * rows_per_head, 1), 0) // rows_per_head
    col = jnp.full(r.shape, sink_ref[h * Q_PER_KV], f32)
    for g in range(1, Q_PER_KV):
        col = jnp.where(r == g, sink_ref[h * Q_PER_KV + g], col)
    return col


def _stack_heads(q, h):
    return jnp.concatenate(
        [q[:, (h * Q_PER_KV + g) * HEAD_DIM:(h * Q_PER_KV + g + 1) * HEAD_DIM] for g in range(Q_PER_KV)], axis=0)


def _nt(a, b):
    return lax.dot_general(a, b, (((1,), (1,)), ((), ())), preferred_element_type=f32)


def _ctx_attn_body(sink_ref, q_ref, k_ref, v_ref, o_ref):
    rows = q_ref.shape[0]
    q = q_ref[...] * (HEAD_DIM ** -0.5)
    k = k_ref[...]
    v = v_ref[...]
    outs = [None] * N_Q_HEADS
    for h in range(N_KV_HEADS):
        qh = _stack_heads(q, h).astype(bf16)
        kh = k[:, h * HEAD_DIM:(h + 1) * HEAD_DIM].astype(bf16)
        vh = v[:, h * HEAD_DIM:(h + 1) * HEAD_DIM].astype(bf16)
        s = _nt(qh, kh)
        sk = _sink_column(sink_ref, h, rows)
        m = jnp.maximum(jnp.max(s, axis=-1, keepdims=True), sk)
        p = jnp.exp(s - m)
        den = jnp.sum(p, axis=-1, keepdims=True) + jnp.exp(sk - m)
        o = jnp.dot(p.astype(bf16), vh, preferred_element_type=f32) / den
        for g in range(Q_PER_KV):
            outs[h * Q_PER_KV + g] = o[g * rows:(g + 1) * rows]
    o_ref[...] = jnp.concatenate(outs, axis=1).astype(o_ref.dtype)


def _context_attention(lay, qkv, sink):
    sp = lay.sp
    kcol = Q_DIM // KV_DIM
    return pl.pallas_call(
        _ctx_attn_body,
        grid=(lay.bp,),
        in_specs=[pl.BlockSpec(memory_space=pltpu.SMEM),
                  pl.BlockSpec((sp, Q_DIM), lambda b: (b, 0)),
                  pl.BlockSpec((sp, KV_DIM), lambda b: (b, kcol)),
                  pl.BlockSpec((sp, KV_DIM), lambda b: (b, kcol + 1))],
        out_specs=pl.BlockSpec((sp, Q_DIM), lambda b: (b, 0)),
        out_shape=jax.ShapeDtypeStruct((lay.n_prompt, Q_DIM), bf16),
        compiler_params=_params(1),
        name="ctx_attn",
    )(sink, qkv, qkv, qkv)


def _rope_tables(t_len):
    t = jnp.arange(t_len)
    row = (t // GRID_W).astype(f32)
    col = (t % GRID_W).astype(f32)
    quarter = HEAD_DIM // 4
    d = jnp.arange(LANES) % HEAD_DIM
    inv = ROPE_BASE ** (-(d % quarter).astype(f32) / quarter)
    pos = jnp.where((d < HEAD_DIM // 2)[None, :], row[:, None], col[:, None])
    ang = pos * inv[None, :]
    cs, sn = jnp.cos(ang), jnp.sin(ang)
    first = ((d % (2 * quarter)) < quarter)[None, :]
    return cs, jnp.where(first, -sn, 0.0), jnp.where(first, 0.0, sn)


def _rope(x, cs, sa, sb):
    quarter = HEAD_DIM // 4
    outs = []
    for c in range(x.shape[1] // LANES):
        xc = x[:, c * LANES:(c + 1) * LANES]
        outs.append(xc * cs + pltpu.roll(xc, LANES - quarter, axis=1) * sa + pltpu.roll(xc, quarter, axis=1) * sb)
    return outs[0] if len(outs) == 1 else jnp.concatenate(outs, axis=1)


def _lat_attn_body(sink_ref, q_ref, k_ref, v_ref, kc_ref, vc_ref, cs_ref, sa_ref, sb_ref, o_ref):
    j = pl.program_id(1)
    t_len = k_ref.shape[0]
    win = 3 * ATT_Q
    start = pl.multiple_of(jnp.clip((j - 1) * ATT_Q, 0, t_len - win), ATT_Q)
    q0 = pl.multiple_of(j * ATT_Q, ATT_Q)
    q = _rope(q_ref[...], cs_ref[pl.ds(q0, ATT_Q), :], sa_ref[pl.ds(q0, ATT_Q), :],
              sb_ref[pl.ds(q0, ATT_Q), :]) * (HEAD_DIM ** -0.5)
    kw = _rope(k_ref[pl.ds(start, win), :], cs_ref[pl.ds(start, win), :], sa_ref[pl.ds(start, win), :],
               sb_ref[pl.ds(start, win), :])
    vw = v_ref[pl.ds(start, win), :]
    kc = kc_ref[...]
    vc = vc_ref[...]
    rows = Q_PER_KV * ATT_Q
    qpos = q0 + lax.broadcasted_iota(jnp.int32, (rows, win), 0) % ATT_Q
    kpos = start + lax.broadcasted_iota(jnp.int32, (rows, win), 1)
    valid = jnp.abs(qpos - kpos) <= WINDOW
    outs = [None] * N_Q_HEADS
    for h in range(N_KV_HEADS):
        sl = slice(h * HEAD_DIM, (h + 1) * HEAD_DIM)
        qh = _stack_heads(q, h).astype(bf16)
        s_c = _nt(qh, kc[:, sl].astype(bf16))
        s_l = jnp.where(valid, _nt(qh, kw[:, sl].astype(bf16)), NEG)
        sk = _sink_column(sink_ref, h, ATT_Q)
        m = jnp.maximum(jnp.maximum(jnp.max(s_c, axis=-1, keepdims=True),
                                    jnp.max(s_l, axis=-1, keepdims=True)), sk)
        p_c = jnp.exp(s_c - m)
        p_l = jnp.exp(s_l - m)
        den = jnp.sum(p_c, axis=-1, keepdims=True) + jnp.sum(p_l, axis=-1, keepdims=True) + jnp.exp(sk - m)
        o = (jnp.dot(p_c.astype(bf16), vc[:, sl].astype(bf16), preferred_element_type=f32)
             + jnp.dot(p_l.astype(bf16), vw[:, sl].astype(bf16), preferred_element_type=f32)) / den
        for g in range(Q_PER_KV):
            outs[h * Q_PER_KV + g] = o[g * ATT_Q:(g + 1) * ATT_Q]
    o_ref[...] = jnp.concatenate(outs, axis=1).astype(o_ref.dtype)


def _latent_attention(lay, qkv, kc, vc, sink):
    ss = lay.ss
    past = kc.shape[1]
    nq = ss // ATT_Q
    q_base = lay.n_prompt // ATT_Q
    s_base = lay.n_prompt // ss
    assert lay.n_prompt % ss == 0
    kcol = Q_DIM // KV_DIM
    cs, sa, sb = _rope_tables(ss)
    return pl.pallas_call(
        _lat_attn_body,
        grid=(lay.bs, nq),
        in_specs=[pl.BlockSpec(memory_space=pltpu.SMEM),
                  pl.BlockSpec((ATT_Q, Q_DIM), lambda b, j: (q_base + b * nq + j, 0)),
                  pl.BlockSpec((ss, KV_DIM), lambda b, j: (s_base + b, kcol)),
                  pl.BlockSpec((ss, KV_DIM), lambda b, j: (s_base + b, kcol + 1)),
                  pl.BlockSpec((None, past, KV_DIM), lambda b, j: (b, 0, 0)),
                  pl.BlockSpec((None, past, KV_DIM), lambda b, j: (b, 0, 0)),
                  _const_spec((ss, LANES)), _const_spec((ss, LANES)), _const_spec((ss, LANES))],
        out_specs=pl.BlockSpec((ATT_Q, Q_DIM), lambda b, j: (b * nq + j, 0)),
        out_shape=jax.ShapeDtypeStruct((lay.bs * ss, Q_DIM), bf16),
        compiler_params=_params(2),
        name="lat_attn",
    )(sink, qkv, qkv, qkv, kc, vc, cs, sa, sb)


CONV_HALO = 16
CONV_SUB = 32


def _dwconv_body(lay, u_ref, prev_ref, next_ref, w_ref, b_ref, g_ref, be_ref, o_ref, buf_ref):
    i = pl.program_id(0)
    pos, cnt = lay.seq_pos(i)
    buf_ref[pl.ds(0, CONV_HALO), :] = jnp.where(pos > 0, prev_ref[...], 0.0)
    buf_ref[pl.ds(CONV_HALO, ROW_TILE), :] = u_ref[...]
    buf_ref[pl.ds(CONV_HALO + ROW_TILE, CONV_HALO), :] = jnp.where(pos < cnt - 1, next_ref[...], 0.0)
    w = w_ref[...]
    off = CONV_HALO - CONV_PAD

    for s in range(ROW_TILE // CONV_SUB):
        r0 = s * CONV_SUB
        acc = jnp.zeros((CONV_SUB, D), f32)
        for t in range(CONV_WIDTH):
            acc = acc + buf_ref[pl.ds(r0 + off + t, CONV_SUB), :] * w[t:t + 1, :]
        z = _layer_norm(acc + b_ref[...], g_ref[...], be_ref[...])
        o_ref[pl.ds(r0, CONV_SUB), :] = _silu(z).astype(o_ref.dtype)


def _dwconv_ln_swish(lay, u, w_dw, b_dw, ln_g, ln_b):
    hpt = ROW_TILE // CONV_HALO
    last = lay.n // CONV_HALO - 1
    return pl.pallas_call(
        functools.partial(_dwconv_body, lay),
        grid=(lay.n_tiles,),
        in_specs=[lay.row_spec(),
                  pl.BlockSpec((CONV_HALO, D), lambda i: (jnp.maximum(i * hpt - 1, 0), 0)),
                  pl.BlockSpec((CONV_HALO, D), lambda i: (jnp.minimum((i + 1) * hpt, last), 0)),
                  _const_spec((CONV_WIDTH, D)), _const_spec((1, D)), _const_spec((1, D)), _const_spec((1, D))],
        out_specs=lay.row_spec(),
        out_shape=jax.ShapeDtypeStruct((lay.n, D), bf16),
        scratch_shapes=[pltpu.VMEM((ROW_TILE + 2 * CONV_HALO, D), f32)],
        compiler_params=_params(1),
        name="dwconv",
    )(u, u, u, w_dw, b_dw.reshape(1, D), ln_g.reshape(1, D), ln_b.reshape(1, D))


def _conformer(lay, x, m, w_pw1, b_pw1, w_dw, b_dw, ln_g, ln_b, w_pw2, b_pw2):
    u = _mod_proj(_WideLayout(lay), x, m[0], m[1], w_pw1, b_pw1, glu=True, name="conv_pw1_glu")
    return _dwconv_ln_swish(lay, u, w_dw, b_dw, ln_g, ln_b)


LORA_PAD = 128
GATE_PAD = 256


def _bdot(a, w):
    return jnp.dot(a.astype(bf16), w, preferred_element_type=f32)


def _rwkv_proj_body(lay, x_ref, prev_ref, next_ref, sh_ref, sc_ref, mu_ref, wrkv_ref, w0_ref, w1_ref, w2_ref,
                    a0_ref, a1_ref, a2_ref, g1_ref, g2_ref, r_ref, k_ref, v_ref, g_ref, lw_ref, a_ref):
    i = pl.program_id(0)
    pos, cnt = lay.seq_pos(i)
    sc = 1.0 + sc_ref[...]
    sh = sh_ref[...]
    h = x_ref[...] * sc + sh
    hb = prev_ref.shape[0]
    h_prev = jnp.where(pos > 0, prev_ref[hb - 1:hb, :] * sc + sh, 0.0)
    h_next = jnp.where(pos < cnt - 1, next_ref[0:1, :] * sc + sh, 0.0)
    row = lax.broadcasted_iota(jnp.int32, (ROW_TILE, 1), 0)
    below = jnp.where(row == 0, h_prev, pltpu.roll(h, 1, axis=0))
    above = jnp.where(row == ROW_TILE - 1, h_next, pltpu.roll(h, ROW_TILE - 1, axis=0))
    xx = 0.5 * (below + above) - h

    def mix(j):
        return (h + xx * mu_ref[j:j + 1, :]).astype(bf16)

    r_ref[...] = _bdot(mix(0), wrkv_ref[0])
    k_ref[...] = _bdot(mix(2), wrkv_ref[1])
    v_ref[...] = _bdot(mix(3), wrkv_ref[2])
    g_ref[...] = _bdot(jax.nn.sigmoid(_bdot(mix(5), g1_ref[...])), g2_ref[...])
    xw = mix(1)
    xa = mix(4)
    for d in range(2):
        z = w0_ref[d:d + 1, :] + _bdot(jnp.tanh(_bdot(xw, w1_ref[d])), w2_ref[d])
        softplus = jnp.maximum(-z, 0.0) + jnp.log(1.0 + jnp.exp(-jnp.abs(z)))
        lw_ref[d] = -jnp.exp(-softplus - 0.5)
        a_ref[d] = jax.nn.sigmoid(a0_ref[d:d + 1, :] + _bdot(_bdot(xa, a1_ref[d]), a2_ref[d]))


def _pad_to(w, axis, size):
    pad = [(0, 0)] * w.ndim
    pad[axis] = (0, size - w.shape[axis])
    return jnp.pad(w, pad)


def _rwkv_project(lay, x, m, mu, w_rkv, w0, w1, w2, a0, a1, a2, g1, g2):
    halo = 8
    hpt = ROW_TILE // halo
    last = lay.n // halo - 1
    nd = (lay.n, D)
    consts = [mu, w_rkv.astype(bf16), w0,
              _pad_to(w1, 2, LORA_PAD).astype(bf16), _pad_to(w2, 1, LORA_PAD).astype(bf16), a0,
              _pad_to(a1, 2, LORA_PAD).astype(bf16), _pad_to(a2, 1, LORA_PAD).astype(bf16),
              _pad_to(g1, 1, GATE_PAD).astype(bf16), _pad_to(g2, 0, GATE_PAD).astype(bf16)]
    return pl.pallas_call(
        functools.partial(_rwkv_proj_body, lay),
        grid=(lay.n_tiles,),
        in_specs=[lay.row_spec(),
                  pl.BlockSpec((halo, D), lambda i: (jnp.maximum(i * hpt - 1, 0), 0)),
                  pl.BlockSpec((halo, D), lambda i: (jnp.minimum((i + 1) * hpt, last), 0)),
                  lay.mod_spec(), lay.mod_spec()] + [_const_spec(c.shape) for c in consts],
        out_specs=[lay.row_spec()] * 4 + [pl.BlockSpec((2, ROW_TILE, D), lambda i: (0, i, 0))] * 2,
        out_shape=[jax.ShapeDtypeStruct(nd, f32)] * 4 + [jax.ShapeDtypeStruct((2,) + nd, f32)] * 2,
        compiler_params=_params(1),
        name="rwkv_proj",
    )(x, x, x, m[0], m[1], *consts)


def _split_dot(x, c, parts):
    acc = None
    for _ in range(parts):
        hi = x.astype(bf16)
        t = jnp.dot(hi, c, preferred_element_type=f32)
        acc = t if acc is None else acc + t
        x = x - hi.astype(f32)
    return acc


def _split_dot_left(c, x, parts):
    acc = None
    for _ in range(parts):
        hi = x.astype(bf16)
        t = jnp.dot(c, hi, preferred_element_type=f32)
        acc = t if acc is None else acc + t
        x = x - hi.astype(f32)
    return acc


def _mm(a, b):
    return jnp.dot(a.astype(bf16), b.astype(bf16), preferred_element_type=f32)


def _split3(x):
    x1 = x.astype(bf16)
    r1 = x - x1.astype(f32)
    x2 = r1.astype(bf16)
    x3 = (r1 - x2.astype(f32)).astype(bf16)
    return x1, x2, x3


def _mm_f32(a3, b):
    b1, b2, b3 = _split3(b)
    n = a3[0].shape[0]
    r1 = jnp.dot(jnp.concatenate(a3, axis=0), b1, preferred_element_type=f32)
    r2 = jnp.dot(jnp.concatenate(a3[:2], axis=0), b2, preferred_element_type=f32)
    r3 = jnp.dot(a3[0], b3, preferred_element_type=f32)
    small = r1[2 * n:] + r2[n:] + r3
    return r1[:n] + ((r1[n:2 * n] + r2[:n]) + small)


def _mm_nt(a, b):
    return lax.dot_general(a.astype(bf16), b.astype(bf16), (((1,), (1,)), ((), ())), preferred_element_type=f32)


def _scan_body(lay, rev, r_ref, k_ref, v_ref, lw_ref, a_ref, kk_ref, ka_ref, rk_ref, s0_ref, tri_ref, bd_ref,
               y_ref, bon_ref, sfin_ref, state_ref):
    t_idx = pl.program_id(1)
    i = (lay.n_tiles - 1 - t_idx) if rev else t_idx
    pos, cnt = lay.seq_pos(i)
    first = (pos == cnt - 1) if rev else (pos == 0)
    final = (pos == 0) if rev else (pos == cnt - 1)
    c_len = SCAN_CHUNK
    n_chunks = SCAN_ROWS // c_len

    @pl.when(first)
    def _():
        state_ref[...] = s0_ref[...]

    lane = lax.broadcasted_iota(jnp.int32, (1, LANES), 1)
    head0 = lane < RWKV_HEAD
    rr = lax.broadcasted_iota(jnp.int32, (2 * c_len, 2 * c_len), 0)
    cc = lax.broadcasted_iota(jnp.int32, (2 * c_len, 2 * c_len), 1)
    same = (rr // c_len) == (cc // c_len)
    tt = rr % c_len
    ss = cc % c_len
    strict = same & ((tt < ss) if rev else (tt > ss))
    incl = same & ((tt <= ss) if rev else (tt >= ss))
    tri = tri_ref[...]
    bd = bd_ref[...]

    def stack(xc):
        return jnp.concatenate([jnp.where(head0, xc, 0.0), jnp.where(head0, 0.0, xc)], axis=0)

    def pair_chunk(rows, p):
        cols = slice(p * LANES, (p + 1) * LANES)
        r = r_ref[rows, cols]
        k = k_ref[rows, cols]
        v = v_ref[rows, cols]
        lw = lw_ref[rows, cols]
        a = a_ref[rows, cols]
        kk = k * kk_ref[:, cols]
        nrm = jnp.sqrt(_split_dot(kk * kk, bd, 2))
        kk = kk / jnp.maximum(nrm, 1e-12)
        b = kk * a
        kd = k * (1.0 + (a - 1.0) * ka_ref[:, cols])
        bon_ref[rows, cols] = _split_dot(r * kd * rk_ref[:, cols], bd, 2) * v
        cum = _split_dot_left(tri, lw, 3)
        tot = cum[0:1, :] if rev else cum[c_len - 1:c_len, :]
        e_tot = jnp.exp(tot)
        e_neg = jnp.exp(-cum)
        a_t = -kk * jnp.exp(cum - lw)
        r_t = r * jnp.exp(cum)
        b_t = b * e_neg
        k_t = kd * e_neg
        ar = jnp.concatenate([stack(a_t), stack(r_t)], axis=0)
        bk = jnp.concatenate([stack(b_t), stack(k_t)], axis=0)
        m = _mm_nt(ar, bk)
        h2 = 2 * c_len
        n_ab = jnp.where(strict, m[:h2, :h2], 0.0)
        a_ak = jnp.where(strict, m[:h2, h2:], 0.0)
        a_rb = jnp.where(incl, m[h2:, :h2], 0.0)
        a_rk = jnp.where(incl, m[h2:, h2:], 0.0)
        state = state_ref[p]
        x0 = _mm_nt(ar, state)
        v_st = stack(v)
        sa = x0[:h2] + _mm(a_ak, v_st)
        pw = n_ab
        span = 1
        while span < c_len:
            pw3 = _split3(pw)
            sa = sa + _mm_f32(pw3, sa)
            span *= 2
            if span < c_len:
                pw = _mm_f32(pw3, pw)
        sav = jnp.concatenate([sa, v_st], axis=0)
        y_st = x0[h2:] + _mm(jnp.concatenate([a_rb, a_rk], axis=1), sav)
        y_ref[rows, cols] = y_st[:c_len] + y_st[c_len:]
        state_ref[p] = state * e_tot + _mm(sav.T, bk * e_tot)

    def chunk(c, carry):
        ce = (n_chunks - 1 - c) if rev else c
        rows = pl.ds(pl.multiple_of(ce * c_len, c_len), c_len)
        for p in range(SCAN_PAIRS):
            pair_chunk(rows, p)
        return carry

    lax.fori_loop(0, n_chunks, chunk, 0)

    @pl.when(final)
    def _():
        sfin_ref[...] = state_ref[...]


def _block_diag_states(s):
    b = s.shape[0]
    s = s.reshape(b, RWKV_HEADS // 2, 2, RWKV_HEAD, RWKV_HEAD)
    eye = jnp.eye(2, dtype=s.dtype)
    out = s[:, :, :, :, None, :] * eye[None, None, :, None, :, None]
    return out.reshape(b, RWKV_HEADS // 2, LANES, LANES)


def _diag_blocks(w):
    b = w.shape[0]
    w = w.reshape(b, RWKV_HEADS // 2, 2, RWKV_HEAD, 2, RWKV_HEAD)
    return jnp.stack([w[:, :, 0, :, 0, :], w[:, :, 1, :, 1, :]], axis=2).reshape(b, RWKV_HEADS, RWKV_HEAD, RWKV_HEAD)


def _rwkv_scan(lay, rev, r, k, v, lw, a, k_k, k_a, r_k, s0):
    d = 1 if rev else 0
    n_seq = lay.bp + lay.bs
    c_len = SCAN_CHUNK
    idx = jnp.arange(c_len)
    tri = ((idx[:, None] <= idx[None, :]) if rev else (idx[:, None] >= idx[None, :])).astype(bf16)
    hh = jnp.arange(LANES) // RWKV_HEAD
    bd = (hh[:, None] == hh[None, :]).astype(bf16)

    def tile(t):
        return (lay.n_tiles - 1 - t) if rev else t

    def seq(t):
        i = tile(t)
        return jnp.where(i < lay.prompt_tiles, i // lay.tiles_per_pseq,
                         lay.bp + (i - lay.prompt_tiles) // lay.tiles_per_sseq)

    width = SCAN_PAIRS * LANES
    row = pl.BlockSpec((SCAN_ROWS, width), lambda p, t: (tile(t), p))
    row_d = pl.BlockSpec((None, SCAN_ROWS, width), lambda p, t: (d, tile(t), p))
    par = pl.BlockSpec((1, width), lambda p, t: (0, p))
    st = pl.BlockSpec((None, SCAN_PAIRS, LANES, LANES), lambda p, t: (seq(t), p, 0, 0))
    return pl.pallas_call(
        functools.partial(_scan_body, lay, rev),
        grid=(RWKV_HEADS // 2 // SCAN_PAIRS, lay.n_tiles),
        in_specs=[row, row, row, row_d, row_d, par, par, par, st,
                  _const_spec((c_len, c_len)), _const_spec((LANES, LANES))],
        out_specs=[row, row, st],
        out_shape=[jax.ShapeDtypeStruct((lay.n, D), f32), jax.ShapeDtypeStruct((lay.n, D), f32),
                   jax.ShapeDtypeStruct((n_seq, RWKV_HEADS // 2, LANES, LANES), f32)],
        scratch_shapes=[pltpu.VMEM((SCAN_PAIRS, LANES, LANES), f32)],
        compiler_params=_params(2),
        name="rwkv_scan_bwd" if rev else "rwkv_scan_fwd",
    )(r, k, v, lw, a, k_k.reshape(1, D), k_a.reshape(1, D), r_k.reshape(1, D), s0, tri, bd)


def _rwkv_out_body(yf_ref, yb_ref, bf_ref, bb_ref, g_ref, gg_ref, gb_ref, bd_ref, o_ref):
    y = yf_ref[...] + yb_ref[...]
    bd = bd_ref[...]
    inv = 1.0 / RWKV_HEAD
    outs = []
    for c in range(D // LANES):
        yc = y[:, c * LANES:(c + 1) * LANES]
        cen = yc - _split_dot(yc, bd, 2) * inv
        var = _split_dot(cen * cen, bd, 2) * inv
        outs.append(cen * lax.rsqrt(var + GN_EPS))
    yn = jnp.concatenate(outs, axis=1) * gg_ref[...] + gb_ref[...]
    o_ref[...] = ((yn + bf_ref[...] + bb_ref[...]) * g_ref[...]).astype(o_ref.dtype)


def _rwkv(lay, x, m, s0_f, s0_b, mu, w_rkv, w0, w1, w2, a0, a1, a2, g1, g2, k_k, k_a, r_k, gn_g, gn_b):
    r, k, v, g, lw, a = _rwkv_project(lay, x, m, mu, w_rkv, w0, w1, w2, a0, a1, a2, g1, g2)
    zeros = jnp.zeros((lay.bp, RWKV_HEADS // 2, LANES, LANES), f32)
    outs = []
    for rev, s0 in ((False, s0_f), (True, s0_b)):
        s_all = jnp.concatenate([zeros, _block_diag_states(s0.astype(f32))], axis=0)
        outs.append(_rwkv_scan(lay, rev, r, k, v, lw, a, k_k, k_a, r_k, s_all))
    hh = jnp.arange(LANES) // RWKV_HEAD
    bd = (hh[:, None] == hh[None, :]).astype(bf16)
    y = pl.pallas_call(
        _rwkv_out_body,
        grid=(lay.n_tiles,),
        in_specs=[lay.row_spec()] * 5 + [_const_spec((1, D)), _const_spec((1, D)), _const_spec((LANES, LANES))],
        out_specs=lay.row_spec(),
        out_shape=jax.ShapeDtypeStruct((lay.n, D), bf16),
        compiler_params=_params(1),
        name="rwkv_groupnorm_gate",
    )(outs[0][0], outs[1][0], outs[0][1], outs[1][1], g, gn_g.reshape(1, D), gn_b.reshape(1, D), bd)
    return y, _diag_blocks(outs[0][2][:lay.bp]), _diag_blocks(outs[1][2][:lay.bp])


def _router_body(x_ref, sh_ref, sc_ref, w_ref, wlo_ref, b_ref, h_ref, route_ref):
    h = x_ref[...] * (1.0 + sc_ref[...]) + sh_ref[...]
    h_hi = h.astype(bf16)
    h_ref[...] = h_hi
    h_lo = (h - h_hi.astype(f32)).astype(bf16)
    lg = (jnp.dot(h_hi, w_ref[...], preferred_element_type=f32)
          + (jnp.dot(h_hi, wlo_ref[...], preferred_element_type=f32)
             + jnp.dot(h_lo, w_ref[...], preferred_element_type=f32))) + b_ref[...]
    lane = lax.broadcasted_iota(jnp.int32, lg.shape, 1)

    def top1(vals):
        mx = jnp.max(vals, axis=-1, keepdims=True)
        idx = jnp.min(jnp.where(vals == mx, lane, LANES), axis=-1, keepdims=True)
        return mx, idx

    gl = jnp.where(lane < N_GROUPS, lg, NEG)
    g_max, g_idx = top1(gl)
    p_g = 1.0 / jnp.sum(jnp.exp(gl - g_max), axis=-1, keepdims=True)
    lo = N_GROUPS + g_idx * EXPERTS_PER_GROUP
    el = jnp.where((lane >= lo) & (lane < lo + EXPERTS_PER_GROUP), lg, NEG)
    m1, i1 = top1(el)
    m2, i2 = top1(jnp.where(lane == i1, NEG, el))
    e2 = jnp.exp(m2 - m1)
    w1 = p_g / (1.0 + e2)
    w2 = p_g * e2 / (1.0 + e2)
    route = jnp.where(lane == 0, (i1 - N_GROUPS).astype(f32),
                      jnp.where(lane == 1, (i2 - N_GROUPS).astype(f32),
                                jnp.where(lane == 2, w1, jnp.where(lane == 3, w2, 0.0))))
    route_ref[...] = route


def _expert_body(be_ref, nu_ref, x_ref, wgu_ref, wd_ref, o_ref, wgu_s, wd_s):
    i = pl.program_id(0)
    used = i < nu_ref[0]

    @pl.when(used & ((i == 0) | (be_ref[i] != be_ref[jnp.maximum(i - 1, 0)])))
    def _():
        wgu_s[...] = wgu_ref[...].astype(bf16)
        wd_s[...] = wd_ref[...].astype(bf16)

    @pl.when(used)
    def _():
        u = jnp.dot(x_ref[...], wgu_s[...], preferred_element_type=f32)
        act = _silu(u[:, :EXPERT_FF]) * u[:, EXPERT_FF:]
        o_ref[...] = jnp.dot(act.astype(bf16), wd_s[...], preferred_element_type=f32).astype(o_ref.dtype)

    @pl.when(jnp.logical_not(used))
    def _():
        o_ref[...] = jnp.zeros_like(o_ref)


def _moe_out_body(y_ref0, y_ref1, route_ref, x_ref, gate_ref, g_ref, be_ref, o_ref):
    route = route_ref[...]
    y = route[:, 2:3] * y_ref0[...].astype(f32) + route[:, 3:4] * y_ref1[...].astype(f32)
    z = ALPHA * x_ref[...] + gate_ref[...] * y
    o_ref[...] = _layer_norm(z, g_ref[...], be_ref[...])


def _moe_layer(lay, x, m, w_group, b_group, w_expert, b_expert, w_gu, w_down, ln_g, ln_b, layer=None):
    if layer is None:
        w_gu, w_down, layer = w_gu[None], w_down[None], 0
    n = lay.n
    n_route = N_GROUPS + N_EXPERTS
    w_r = _pad_to(jnp.concatenate([w_group, w_expert], axis=1).astype(f32), 1, LANES)
    b_r = _pad_to(jnp.concatenate([b_group, b_expert]).astype(f32).reshape(1, n_route), 1, LANES)
    w_hi = w_r.astype(bf16)
    h, route = pl.pallas_call(
        _router_body,
        grid=(lay.n_tiles,),
        in_specs=[lay.row_spec(), lay.mod_spec(), lay.mod_spec(), _const_spec((D, LANES)), _const_spec((D, LANES)),
                  _const_spec((1, LANES))],
        out_specs=[lay.row_spec(), lay.row_spec(LANES)],
        out_shape=[jax.ShapeDtypeStruct((n, D), bf16), jax.ShapeDtypeStruct((n, LANES), f32)],
        compiler_params=_params(1),
        name="moe_router",
    )(x, m[3], m[4], w_hi, (w_r - w_hi.astype(f32)).astype(bf16), b_r)

    s = n * TOP_K
    e_flat = route[:, :TOP_K].astype(jnp.int32).reshape(-1)
    pair = jnp.arange(s, dtype=jnp.int32)
    e_sorted, order = lax.sort((e_flat, pair), num_keys=1, is_stable=True)
    experts = jnp.arange(N_EXPERTS, dtype=jnp.int32)
    counts = jnp.sum((e_flat[:, None] == experts[None, :]).astype(jnp.int32), axis=0)
    padded = (counts + MOE_ROWS - 1) // MOE_ROWS * MOE_ROWS
    pad_end = jnp.cumsum(padded)
    pad_start = pad_end - padded
    start = jnp.cumsum(counts) - counts
    dest_sorted = pad_start[e_sorted] + pair - start[e_sorted]
    n_blocks = -(-s // MOE_ROWS) + N_EXPERTS
    p_rows = n_blocks * MOE_ROWS
    block_start = jnp.arange(n_blocks, dtype=jnp.int32) * MOE_ROWS
    block_expert = jnp.minimum(jnp.sum((pad_end[None, :] <= block_start[:, None]).astype(jnp.int32), axis=1),
                               N_EXPERTS - 1)
    n_used = (pad_end[N_EXPERTS - 1] // MOE_ROWS).astype(jnp.int32).reshape(1)
    slot = jnp.arange(p_rows, dtype=jnp.int32)
    slot_expert = jnp.repeat(block_expert, MOE_ROWS)
    in_run = jnp.minimum(slot - pad_start[slot_expert], jnp.maximum(counts[slot_expert] - 1, 0))
    slot_token = order[jnp.clip(start[slot_expert] + in_run, 0, s - 1)] // TOP_K
    _, dest = lax.sort((order, dest_sorted), num_keys=1)
    dest = dest.reshape(n, TOP_K)

    xb = h.at[slot_token].get(mode="promise_in_bounds")
    yb = pl.pallas_call(
        _expert_body,
        grid_spec=pltpu.PrefetchScalarGridSpec(
            num_scalar_prefetch=2,
            grid=(n_blocks,),
            in_specs=[pl.BlockSpec((MOE_ROWS, D), lambda i, be, nu: (jnp.minimum(i, nu[0] - 1), 0)),
                      pl.BlockSpec((None, None, D, 2 * EXPERT_FF), lambda i, be, nu: (layer, be[i], 0, 0)),
                      pl.BlockSpec((None, None, EXPERT_FF, D), lambda i, be, nu: (layer, be[i], 0, 0))],
            out_specs=pl.BlockSpec((MOE_ROWS, D), lambda i, be, nu: (i, 0)),
            scratch_shapes=[pltpu.VMEM((D, 2 * EXPERT_FF), bf16), pltpu.VMEM((EXPERT_FF, D), bf16)]),
        out_shape=jax.ShapeDtypeStruct((p_rows, D), bf16),
        compiler_params=_params(1),
        name="moe_experts",
    )(block_expert, n_used, xb, w_gu, w_down)
    y0 = yb.at[dest[:, 0]].get(mode="promise_in_bounds")
    y1 = yb.at[dest[:, 1]].get(mode="promise_in_bounds")

    return pl.pallas_call(
        _moe_out_body,
        grid=(lay.n_tiles,),
        in_specs=[lay.row_spec(), lay.row_spec(), lay.row_spec(LANES), lay.row_spec(), lay.mod_spec(),
                  _const_spec((1, D)), _const_spec((1, D))],
        out_specs=lay.row_spec(),
        out_shape=jax.ShapeDtypeStruct((n, D), f32),
        compiler_params=_params(1),
        name="moe_combine_ln",
    )(y0, y1, route, x, m[5], ln_g.reshape(1, D), ln_b.reshape(1, D))


def kernel(x_prompt, x_sample, c, c_ctx, cache_attn_k, cache_attn_v, state_rwkv_fwd, state_rwkv_bwd, mod_w, mod_b, post_ln_g, post_ln_b, attn_w_qkv, attn_w_o, attn_sink, conv_w_pw1, conv_b_pw1, conv_w_dw, conv_b_dw, conv_ln_g, conv_ln_b, conv_w_pw2, conv_b_pw2, rwkv_mu, rwkv_w_rkv, rwkv_w0, rwkv_w1, rwkv_w2, rwkv_a0, rwkv_a1, rwkv_a2, rwkv_g1, rwkv_g2, rwkv_k_k, rwkv_k_a, rwkv_r_k, rwkv_gn_g, rwkv_gn_b, rwkv_w_o, moe_w_group, moe_b_group, moe_w_expert, moe_b_expert, moe_w_gate_up, moe_w_down):
    bp, sp, _ = x_prompt.shape
    bs, ss, _ = x_sample.shape
    assert 1 + bs <= MOD_ROWS
    lay = _Layout(bp, sp, bs, ss)
    wide = _WideLayout(lay)
    x = jnp.concatenate([x_prompt.reshape(bp * sp, D), x_sample.reshape(bs * ss, D)], axis=0)
    cond = jnp.concatenate([c_ctx[None, :], c, jnp.zeros((MOD_ROWS - 1 - bs, D), f32)], axis=0)
    modt = _modulation_table(cond, mod_w, mod_b)
    zero_b = jnp.zeros((D,), f32)
    new_k, new_v, new_sf, new_sb = [], [], [], []
    n_attn = n_conv = n_rwkv = 0
    for i in range(DEPTH):
        m = modt[i]
        kind = i % 3
        if kind == 0:
            j = n_attn
            qkv = _mod_proj(wide, x, m[0], m[1], attn_w_qkv[j], jnp.zeros((Q_DIM + 2 * KV_DIM,), f32), name="attn_qkv")
            past = cache_attn_k.shape[2]
            o_p = _context_attention(lay, qkv, attn_sink[j])
            o_s = _latent_attention(lay, qkv, cache_attn_k[:, j].reshape(bs, past, KV_DIM),
                                    cache_attn_v[:, j].reshape(bs, past, KV_DIM), attn_sink[j])
            y = jnp.concatenate([o_p, o_s], axis=0)
            w_out, b_out = attn_w_o[j], zero_b
            new_k.append(qkv[:lay.n_prompt, Q_DIM:Q_DIM + KV_DIM].reshape(bp, sp, N_KV_HEADS, HEAD_DIM))
            new_v.append(qkv[:lay.n_prompt, Q_DIM + KV_DIM:].reshape(bp, sp, N_KV_HEADS, HEAD_DIM))
            n_attn += 1
        elif kind == 1:
            j = n_conv
            y = _conformer(lay, x, m, conv_w_pw1[j], conv_b_pw1[j], conv_w_dw[j], conv_b_dw[j], conv_ln_g[j],
                           conv_ln_b[j], conv_w_pw2[j], conv_b_pw2[j])
            w_out, b_out = conv_w_pw2[j], conv_b_pw2[j]
            n_conv += 1
        else:
            j = n_rwkv
            y, sf, sb = _rwkv(lay, x, m, state_rwkv_fwd[:, j], state_rwkv_bwd[:, j], rwkv_mu[j], rwkv_w_rkv[j],
                              rwkv_w0[j], rwkv_w1[j], rwkv_w2[j], rwkv_a0[j], rwkv_a1[j], rwkv_a2[j], rwkv_g1[j],
                              rwkv_g2[j], rwkv_k_k[j], rwkv_k_a[j], rwkv_r_k[j], rwkv_gn_g[j], rwkv_gn_b[j])
            w_out, b_out = rwkv_w_o[j], zero_b
            new_sf.append(sf)
            new_sb.append(sb)
            n_rwkv += 1
        x = _out_proj_ln(wide, y, w_out, b_out, x, m[2], post_ln_g[i, 0], post_ln_b[i, 0], name="mixer_out_ln")
        x = _moe_layer(wide, x, m, moe_w_group[i], moe_b_group[i], moe_w_expert[i], moe_b_expert[i],
                       moe_w_gate_up, moe_w_down, post_ln_g[i, 1], post_ln_b[i, 1], layer=i)
    y_prompt = x[:lay.n_prompt].reshape(bp, sp, D)
    y_sample = x[lay.n_prompt:].reshape(bs, ss, D)
    return (y_prompt, y_sample, jnp.stack(new_k, axis=1), jnp.stack(new_v, axis=1),
            jnp.stack(new_sf, axis=1).astype(x_prompt.dtype), jnp.stack(new_sb, axis=1).astype(x_prompt.dtype))
```
